```python
import jax, jax.numpy as jnp
from jax import lax
import numpy as np

D_MODEL = 1024
BATCH = 8
SEQ = 4096
DEPTH = 4

CHUNK = 64
A_HEADS = 8
A_HEAD_DIM = 64
A_WIDTH = A_HEADS * A_HEAD_DIM
A_LEFT_CHUNKS = 8
A_BAND = (A_LEFT_CHUNKS + 1) * CHUNK
REL_CLIP = 128
B_BLOCK = 128
B_GROUPS = 4
B_WIDTH = D_MODEL // 2
B_GROUP_DIM = B_WIDTH // B_GROUPS
C_WIDTH = D_MODEL // 2
C_BLOCKS = 8
C_BLOCK_DIM = C_WIDTH // C_BLOCKS
CONV_WIDTH = 4
LRU_C = 8.0
N_BRANCH = 3
IN_COLS = 3 * A_WIDTH + 2 * B_WIDTH + 2 * C_WIDTH + N_BRANCH * D_MODEL
N_GROUPS = 4
EXPERTS_PER_GROUP = 4
N_EXPERTS = N_GROUPS * EXPERTS_PER_GROUP
TOP_K = 2
D_EXPERT = D_MODEL // 2
EPS = 1e-6

kernel_name = 'hybrid_streaming_gated_trunk'


def rms_norm(x, g):
    x32 = x.astype(jnp.float32)
    y = x32 * lax.rsqrt(jnp.mean(x32 * x32, axis=-1, keepdims=True) + EPS)
    return (y * g.astype(jnp.float32)).astype(x.dtype)


def chunked_rel_attention(q, k, v, rel_bias):
    b, s, _ = q.shape
    nc = s // CHUNK
    shp = (b, nc, CHUNK, A_HEADS, A_HEAD_DIM)
    q = q.reshape(shp)
    pad = ((0, 0), (A_LEFT_CHUNKS, 0), (0, 0), (0, 0), (0, 0))
    kp = jnp.pad(k.reshape(shp), pad)
    vp = jnp.pad(v.reshape(shp), pad)
    k_band = jnp.concatenate([kp[:, j:j + nc] for j in range(A_LEFT_CHUNKS + 1)], axis=2)
    v_band = jnp.concatenate([vp[:, j:j + nc] for j in range(A_LEFT_CHUNKS + 1)], axis=2)
    scores = jnp.einsum('bcqhd,bckhd->bhcqk', q, k_band).astype(jnp.float32) * (A_HEAD_DIM ** -0.5)
    q_pos = A_LEFT_CHUNKS * CHUNK + jnp.arange(CHUNK)
    k_pos = jnp.arange(A_BAND)
    rel = jnp.clip(q_pos[:, None] - k_pos[None, :], -REL_CLIP, REL_CLIP) + REL_CLIP
    bias = rel_bias.astype(jnp.float32)[:, rel]
    key_chunk = jnp.arange(nc)[:, None] - A_LEFT_CHUNKS + k_pos[None, :] // CHUNK
    valid = key_chunk >= 0
    scores = jnp.where(valid[None, None, :, None, :], scores + bias[None, :, None], -1e30)
    probs = jax.nn.softmax(scores, axis=-1).astype(v.dtype)
    out = jnp.einsum('bhcqk,bckhd->bcqhd', probs, v_band)
    return out.reshape(b, s, A_WIDTH)


def chunked_spatial_gating(u, v, norm_g, w_s, b_s):
    b, s, _ = u.shape
    nb = s // B_BLOCK
    u = jax.nn.gelu(u)
    v = rms_norm(jax.nn.gelu(v), norm_g).reshape(b, nb, B_BLOCK, B_GROUPS, B_GROUP_DIM)
    causal = jnp.tril(jnp.ones((B_BLOCK, B_BLOCK), dtype=bool))
    w = jnp.where(causal[None], w_s, jnp.zeros_like(w_s))
    mixed = jnp.einsum('gts,bnsgc->bntgc', w, v) + b_s.T[None, None, :, :, None]
    return u * mixed.reshape(b, s, B_WIDTH)


def causal_depthwise_conv(x, w, bias):
    s = x.shape[1]
    xp = jnp.pad(x, ((0, 0), (CONV_WIDTH - 1, 0), (0, 0)))
    return sum(xp[:, j:j + s] * w[j] for j in range(CONV_WIDTH)) + bias


def block_diag_linear(x, w, bias):
    b, s, _ = x.shape
    y = jnp.einsum('bshi,hij->bshj', x.reshape(b, s, C_BLOCKS, C_BLOCK_DIM), w)
    return y.reshape(b, s, C_WIDTH) + bias


def _lin_rec_combine(left, right):
    a1, b1 = left
    a2, b2 = right
    return a1 * a2, a2 * b1 + b2


def rg_lru(x, w_a, b_a, w_x, b_x, lam):
    r = jax.nn.sigmoid(block_diag_linear(x, w_a, b_a)).astype(jnp.float32)
    i = jax.nn.sigmoid(block_diag_linear(x, w_x, b_x))
    log_a = -LRU_C * r * jax.nn.softplus(-lam.astype(jnp.float32))
    a = jnp.exp(log_a)
    mult = jnp.sqrt(jnp.maximum(-jnp.expm1(2.0 * log_a), 0.0))
    bterm = mult * (i * x).astype(jnp.float32)
    _, h = lax.associative_scan(_lin_rec_combine, (a, bterm), axis=1)
    return h.astype(x.dtype)


def hybrid_mixer(xn, w_in, b_gate, rel_bias, sgu_norm_g, sgu_w, sgu_b, conv_w, conv_b,
                 rg_wa, rg_ba, rg_wx, rg_bx, rg_lambda, w_branch, w_out):
    b, s, _ = xn.shape
    proj = xn @ w_in
    cuts = [A_WIDTH, 2 * A_WIDTH, 3 * A_WIDTH,
            3 * A_WIDTH + B_WIDTH, 3 * A_WIDTH + 2 * B_WIDTH,
            3 * A_WIDTH + 2 * B_WIDTH + C_WIDTH, 3 * A_WIDTH + 2 * B_WIDTH + 2 * C_WIDTH]
    q, k, v, u, sv, xg, xr, gl = jnp.split(proj, cuts, axis=-1)
    a_out = chunked_rel_attention(q, k, v, rel_bias)
    b_out = chunked_spatial_gating(u, sv, sgu_norm_g, sgu_w, sgu_b)
    c_out = jax.nn.gelu(xg) * rg_lru(causal_depthwise_conv(xr, conv_w, conv_b),
                                     rg_wa, rg_ba, rg_wx, rg_bx, rg_lambda)
    gates = jax.nn.sigmoid(gl.reshape(b, s, N_BRANCH, D_MODEL) + b_gate)
    merged = (gates[:, :, 0] * (a_out @ w_branch[0])
              + gates[:, :, 1] * (b_out @ w_branch[1])
              + gates[:, :, 2] * (c_out @ w_branch[2]))
    return merged @ w_out


def hierarchical_moe(x, wg, bg, we, be, w_gate, w_up, w_down):
    b, s, d = x.shape
    xt = x.reshape(b * s, d)
    g_logits = (xt @ wg + bg).astype(jnp.float32)
    g_probs = jax.nn.softmax(g_logits, axis=-1)
    g_idx = jnp.argmax(g_logits, axis=-1)
    g_w = jnp.take_along_axis(g_probs, g_idx[:, None], axis=-1)
    e_logits = (xt @ we + be).astype(jnp.float32).reshape(-1, N_GROUPS, EXPERTS_PER_GROUP)
    e_logits = jnp.take_along_axis(e_logits, g_idx[:, None, None], axis=1)[:, 0]
    e_probs = jax.nn.softmax(e_logits, axis=-1)
    top_p, top_i = lax.top_k(e_probs, TOP_K)
    top_p = top_p / jnp.sum(top_p, axis=-1, keepdims=True)
    expert_id = g_idx[:, None] * EXPERTS_PER_GROUP + top_i
    combine = jnp.sum(jax.nn.one_hot(expert_id, N_EXPERTS, dtype=jnp.float32)
                      * (g_w * top_p)[..., None], axis=1).astype(x.dtype)
    y = jnp.zeros_like(xt)
    for e in range(N_EXPERTS):
        h = jax.nn.silu(xt @ w_gate[e]) * (xt @ w_up[e])
        y = y + combine[:, e:e + 1] * (h @ w_down[e])
    return y.reshape(b, s, d)


def setup_inputs(seed: int = 0) -> dict:
    key = jax.random.key(seed)
    ks = jax.random.split(key, 26)
    f32 = jnp.float32
    L = DEPTH

    def nrm(k, shape, scale):
        return jax.random.normal(k, shape, f32) * scale

    u_lam = jax.random.uniform(ks[14], (L, C_WIDTH), f32, 0.9, 0.999)
    a0 = u_lam ** (1.0 / LRU_C)
    rg_lambda = jnp.log(a0) - jnp.log1p(-a0)
    return {
        'x': nrm(ks[0], (BATCH, SEQ, D_MODEL), 1.0),
        'norm1_g': 1.0 + nrm(ks[1], (L, D_MODEL), 0.02),
        'w_in': nrm(ks[2], (L, D_MODEL, IN_COLS), D_MODEL ** -0.5),
        'b_gate': nrm(ks[3], (L, N_BRANCH, D_MODEL), 0.02),
        'rel_bias': nrm(ks[4], (L, A_HEADS, 2 * REL_CLIP + 1), 0.1),
        'sgu_norm_g': 1.0 + nrm(ks[5], (L, B_WIDTH), 0.02),
        'sgu_w': nrm(ks[6], (L, B_GROUPS, B_BLOCK, B_BLOCK), 0.5 * B_BLOCK ** -0.5),
        'sgu_b': 1.0 + nrm(ks[7], (L, B_GROUPS, B_BLOCK), 0.02),
        'conv_w': nrm(ks[8], (L, CONV_WIDTH, C_WIDTH), CONV_WIDTH ** -0.5),
        'conv_b': nrm(ks[9], (L, C_WIDTH), 0.02),
        'rg_wa': nrm(ks[10], (L, C_BLOCKS, C_BLOCK_DIM, C_BLOCK_DIM), C_BLOCK_DIM ** -0.5),
        'rg_ba': nrm(ks[11], (L, C_WIDTH), 0.02),
        'rg_wx': nrm(ks[12], (L, C_BLOCKS, C_BLOCK_DIM, C_BLOCK_DIM), C_BLOCK_DIM ** -0.5),
        'rg_bx': nrm(ks[13], (L, C_WIDTH), 0.02),
        'rg_lambda': rg_lambda,
        'w_branch': nrm(ks[15], (L, N_BRANCH, A_WIDTH, D_MODEL), A_WIDTH ** -0.5),
        'w_out': nrm(ks[16], (L, D_MODEL, D_MODEL), D_MODEL ** -0.5),
        'norm2_g': 1.0 + nrm(ks[17], (L, D_MODEL), 0.02),
        'router_grp_w': nrm(ks[18], (L, D_MODEL, N_GROUPS), D_MODEL ** -0.5),
        'router_grp_b': nrm(ks[19], (L, N_GROUPS), 0.01),
        'router_exp_w': nrm(ks[20], (L, D_MODEL, N_EXPERTS), D_MODEL ** -0.5),
        'router_exp_b': nrm(ks[21], (L, N_EXPERTS), 0.01),
        'exp_w_gate': nrm(ks[22], (L, N_EXPERTS, D_MODEL, D_EXPERT), D_MODEL ** -0.5),
        'exp_w_up': nrm(ks[23], (L, N_EXPERTS, D_MODEL, D_EXPERT), D_MODEL ** -0.5),
        'exp_w_down': nrm(ks[24], (L, N_EXPERTS, D_EXPERT, D_MODEL), D_EXPERT ** -0.5),
        'final_norm_g': 1.0 + nrm(ks[25], (D_MODEL,), 0.02),
    }


def reference(x, norm1_g, w_in, b_gate, rel_bias, sgu_norm_g, sgu_w, sgu_b, conv_w, conv_b,
              rg_wa, rg_ba, rg_wx, rg_bx, rg_lambda, w_branch, w_out, norm2_g,
              router_grp_w, router_grp_b, router_exp_w, router_exp_b,
              exp_w_gate, exp_w_up, exp_w_down, final_norm_g):
    for l in range(DEPTH):
        xn = rms_norm(x, norm1_g[l])
        x = x + hybrid_mixer(xn, w_in[l], b_gate[l], rel_bias[l], sgu_norm_g[l], sgu_w[l], sgu_b[l],
                             conv_w[l], conv_b[l], rg_wa[l], rg_ba[l], rg_wx[l], rg_bx[l],
                             rg_lambda[l], w_branch[l], w_out[l])
        xn = rms_norm(x, norm2_g[l])
        x = x + hierarchical_moe(xn, router_grp_w[l], router_grp_b[l], router_exp_w[l], router_exp_b[l],
                                 exp_w_gate[l], exp_w_up[l], exp_w_down[l])
    return rms_norm(x, final_norm_g)
```

```python
import functools

import jax
import jax.numpy as jnp
from jax import lax
from jax.experimental import pallas as pl
from jax.experimental.pallas import tpu as pltpu

F32 = jnp.float32
BF16 = jnp.bfloat16

D_MODEL = 1024
CHUNK = 64
A_HEADS = 8
A_HEAD_DIM = 64
A_WIDTH = 512
A_LEFT_CHUNKS = 8
REL_CLIP = 128
B_BLOCK = 128
B_GROUPS = 4
B_WIDTH = 512
C_WIDTH = 512
C_BLOCKS = 8
CONV_WIDTH = 4
LRU_C = 8.0
N_BRANCH = 3
GATE_COLS = N_BRANCH * D_MODEL
IN_COLS = 3 * A_WIDTH + 2 * B_WIDTH + 2 * C_WIDTH + GATE_COLS
N_GROUPS = 4
EXPERTS_PER_GROUP = 4
N_EXPERTS = 16
D_EXPERT = 512
EPS = 1e-6
NEG = -1e30

COL_BLK = 512
Q_BLK, K_BLK, V_BLK, U_BLK, SV_BLK, XG_BLK, XR_BLK = 6, 7, 8, 9, 10, 11, 12

TM_IN = 512
TQ = 256
TM_MIX = 256
TT = 512
TM_E = 256
N_BUCKETS = N_GROUPS * 6
PAIR_LO = (0, 0, 0, 1, 1, 2)
PAIR_HI = (1, 2, 3, 2, 3, 3)
LANES = 128
VMEM_LIMIT = 56 * 1024 * 1024


def _cparams(sem):
    return pltpu.CompilerParams(dimension_semantics=sem, vmem_limit_bytes=VMEM_LIMIT)


def _rms(x, g):
    return x * lax.rsqrt(jnp.mean(x * x, axis=-1, keepdims=True) + EPS) * g


def _inproj_kernel(x_ref, g_ref, w_ref, o_ref):
    xn = _rms(x_ref[...], g_ref[...]).astype(BF16)
    for j in range(IN_COLS // COL_BLK):
        sl = slice(j * COL_BLK, (j + 1) * COL_BLK)
        o_ref[:, sl] = jnp.dot(xn, w_ref[:, sl], preferred_element_type=F32).astype(BF16)


def _in_proj(x, g, w):
    n = x.shape[0]
    return pl.pallas_call(
        _inproj_kernel,
        grid=(n // TM_IN,),
        in_specs=[
            pl.BlockSpec((TM_IN, D_MODEL), lambda i: (i, 0)),
            pl.BlockSpec((1, D_MODEL), lambda i: (0, 0)),
            pl.BlockSpec((D_MODEL, IN_COLS), lambda i: (0, 0), pipeline_mode=pl.Buffered(1)),
        ],
        out_specs=pl.BlockSpec((TM_IN, IN_COLS), lambda i: (i, 0)),
        out_shape=jax.ShapeDtypeStruct((n, IN_COLS), BF16),
        compiler_params=_cparams(("parallel",)),
        name="in_proj",
    )(x, g, w)


def _attn_kernel(q_ref, k0_ref, k1_ref, k2_ref, v0_ref, v1_ref, v2_ref, bias_ref, o_ref):
    i = pl.program_id(1)
    col = lax.broadcasted_iota(jnp.int32, (TQ, 3 * TQ), 1)
    dead = col < jnp.maximum(2 - i, 0) * TQ
    lane = lax.broadcasted_iota(jnp.int32, (TQ, LANES), 1)
    first = lane < A_HEAD_DIM
    for p in range(A_HEADS // 2):
        sl = slice(p * LANES, (p + 1) * LANES)
        qp = q_ref[:, sl]
        kp = jnp.concatenate([k0_ref[:, sl], k1_ref[:, sl], k2_ref[:, sl]], axis=0)
        vp = jnp.concatenate([v0_ref[:, sl], v1_ref[:, sl], v2_ref[:, sl]], axis=0)
        outs = []
        for hh in range(2):
            qh = jnp.where(first if hh == 0 else jnp.logical_not(first), qp, jnp.zeros_like(qp))
            s = lax.dot_general(qh, kp, (((1,), (1,)), ((), ())), preferred_element_type=F32)
            s = s * (A_HEAD_DIM ** -0.5) + bias_ref[2 * p + hh]
            s = jnp.where(dead, NEG, s)
            e = jnp.exp(s - jnp.max(s, axis=-1, keepdims=True))
            l = jnp.sum(e, axis=-1, keepdims=True)
            outs.append(jnp.dot(e.astype(BF16), vp, preferred_element_type=F32) / l)
        o_ref[:, sl] = jnp.where(first, outs[0], outs[1]).astype(BF16)


def _attention(proj, bias, batch, seq):
    nq = seq // TQ

    def blk(col, back):
        return pl.BlockSpec((TQ, COL_BLK), lambda b, i: (b * nq + jnp.maximum(i - back, 0), col))

    return pl.pallas_call(
        _attn_kernel,
        grid=(batch, nq),
        in_specs=[blk(Q_BLK, 0), blk(K_BLK, 2), blk(K_BLK, 1), blk(K_BLK, 0),
                  blk(V_BLK, 2), blk(V_BLK, 1), blk(V_BLK, 0),
                  pl.BlockSpec((A_HEADS, TQ, 3 * TQ), lambda b, i: (0, 0, 0))],
        out_specs=pl.BlockSpec((TQ, A_WIDTH), lambda b, i: (b * nq + i, 0)),
        out_shape=jax.ShapeDtypeStruct((batch * seq, A_WIDTH), BF16),
        compiler_params=_cparams(("parallel", "parallel")),
        name="attention",
    )(proj, proj, proj, proj, proj, proj, proj, bias)


def _attn_bias_table(rel_bias):
    q = jnp.arange(TQ)
    k = jnp.arange(3 * TQ)
    rel = jnp.clip((2 * TQ + q)[:, None] - k[None, :], -REL_CLIP, REL_CLIP) + REL_CLIP
    qc = (q // CHUNK)[:, None]
    kc = (k // CHUNK)[None, :]
    band = (kc >= qc) & (kc <= qc + A_LEFT_CHUNKS)
    return jnp.where(band[None], rel_bias.astype(F32)[:, rel], NEG)


def _scan_linear(a, b):
    n = a.shape[0]
    rows = lax.broadcasted_iota(jnp.int32, a.shape, 0)
    s = 1
    while s < n:
        keep = rows >= s
        b = a * jnp.where(keep, pltpu.roll(b, s, 0), 0.0) + b
        a = a * jnp.where(keep, pltpu.roll(a, s, 0), 1.0)
        s *= 2
    return a, b


def _mix_kernel(gl_ref, u_ref, sv_ref, xg_ref, xr_ref, att_ref, x_ref,
                bgate_ref, sgn_ref, sgw_ref, sgb_ref, cw_ref, cb_ref, wax_ref, bax_ref, lam_ref,
                wbr_ref, wout_ref, o_ref, ext_ref, h_ref):
    t = pl.program_id(1)

    @pl.when(t == 0)
    def _():
        ext_ref[0:8, :] = jnp.zeros((8, C_WIDTH), F32)
        h_ref[...] = jnp.zeros_like(h_ref)

    u = jax.nn.gelu(u_ref[...].astype(F32))
    v = _rms(jax.nn.gelu(sv_ref[...].astype(F32)), sgn_ref[...]).astype(BF16)
    tri = (lax.broadcasted_iota(jnp.int32, (B_BLOCK, B_BLOCK), 0)
           >= lax.broadcasted_iota(jnp.int32, (B_BLOCK, B_BLOCK), 1))
    blocks = []
    for blk in range(TM_MIX // B_BLOCK):
        rs = slice(blk * B_BLOCK, (blk + 1) * B_BLOCK)
        groups = []
        for g in range(B_GROUPS):
            cs = slice(g * LANES, (g + 1) * LANES)
            w = jnp.where(tri, sgw_ref[g], 0.0).astype(BF16)
            groups.append(jnp.dot(w, v[rs, cs], preferred_element_type=F32))
        blocks.append(jnp.concatenate(groups, axis=1) + sgb_ref[...])
    b_out = (u * jnp.concatenate(blocks, axis=0)).astype(BF16)

    xr = xr_ref[...].astype(F32)
    ext_ref[8:, :] = xr
    xc = cb_ref[...]
    for j in range(CONV_WIDTH):
        xc = xc + cw_ref[j:j + 1, :] * ext_ref[pl.ds(8 - (CONV_WIDTH - 1) + j, TM_MIX), :]
    ext_ref[0:8, :] = xr[TM_MIX - 8:, :]
    ri = jnp.dot(xc.astype(BF16), wax_ref[...], preferred_element_type=F32) + bax_ref[...]
    r = jax.nn.sigmoid(ri[:, :C_WIDTH])
    ig = jax.nn.sigmoid(ri[:, C_WIDTH:])
    z = -lam_ref[...]
    softplus = jnp.maximum(z, 0.0) + jnp.log1p(jnp.exp(-jnp.abs(z)))
    log_a = -LRU_C * r * softplus
    a = jnp.exp(log_a)
    th = jnp.tanh(log_a)
    mult = jnp.sqrt(jnp.maximum(-2.0 * th / (1.0 - th), 0.0))
    a_cum, h = _scan_linear(a, mult * (ig * xc))
    h = a_cum * h_ref[...] + h
    h_ref[...] = h[TM_MIX - 1:, :]
    c_out = (jax.nn.gelu(xg_ref[...].astype(F32)) * h).astype(BF16)

    merged = None
    for k, br in enumerate((att_ref[...], b_out, c_out)):
        cs = slice(k * D_MODEL, (k + 1) * D_MODEL)
        gate = jax.nn.sigmoid(gl_ref[:, cs].astype(F32) + bgate_ref[:, cs])
        term = gate * jnp.dot(br, wbr_ref[k], preferred_element_type=F32)
        merged = term if merged is None else merged + term
    o_ref[...] = x_ref[...] + jnp.dot(merged.astype(BF16), wout_ref[...], preferred_element_type=F32)


def _mix(proj, att, x, p, batch, seq):
    nt = seq // TM_MIX
    n = batch * seq

    def row(width, col):
        return pl.BlockSpec((TM_MIX, width), lambda b, t: (b * nt + t, col))

    def const(shape):
        return pl.BlockSpec(shape, lambda b, t: (0,) * len(shape))

    return pl.pallas_call(
        _mix_kernel,
        grid=(batch, nt),
        in_specs=[row(GATE_COLS, 0), row(COL_BLK, U_BLK), row(COL_BLK, SV_BLK), row(COL_BLK, XG_BLK),
                  row(COL_BLK, XR_BLK), row(A_WIDTH, 0), row(D_MODEL, 0),
                  const((1, GATE_COLS)), const((1, B_WIDTH)), const((B_GROUPS, B_BLOCK, B_BLOCK)),
                  const((B_BLOCK, B_WIDTH)), const((CONV_WIDTH, C_WIDTH)), const((1, C_WIDTH)),
                  const((C_WIDTH, 2 * C_WIDTH)), const((1, 2 * C_WIDTH)), const((1, C_WIDTH)),
                  const((N_BRANCH, A_WIDTH, D_MODEL)), const((D_MODEL, D_MODEL))],
        out_specs=row(D_MODEL, 0),
        out_shape=jax.ShapeDtypeStruct((n, D_MODEL), F32),
        scratch_shapes=[pltpu.VMEM((TM_MIX + 8, C_WIDTH), F32), pltpu.VMEM((1, C_WIDTH), F32)],
        compiler_params=_cparams(("parallel", "arbitrary")),
        name="mix",
    )(proj, proj, proj, proj, proj, att, x,
      p["b_gate"], p["sgu_norm_g"], p["sgu_w"], p["sgu_b"], p["conv_w"], p["conv_b"],
      p["w_ax"], p["b_ax"], p["lam"], p["w_branch"], p["w_out"])


def _route_logits(xn, wr_ref, br_ref):
    return jnp.dot(xn, wr_ref[...], preferred_element_type=F32, precision=lax.Precision.HIGHEST) + br_ref[...]


def _router_kernel(x_ref, g_ref, wr_ref, br_ref, xn_ref, meta_ref, cnt_ref, carry_ref):
    @pl.when(pl.program_id(0) == 0)
    def _():
        carry_ref[...] = jnp.zeros_like(carry_ref)

    xn = _rms(x_ref[...], g_ref[...])
    xn_ref[...] = xn
    lg = _route_logits(xn, wr_ref, br_ref)
    lane = lax.broadcasted_iota(jnp.int32, (TT, LANES), 1).astype(F32)

    def top(mask):
        best = jnp.max(jnp.where(mask, lg, -jnp.inf), axis=-1, keepdims=True)
        idx = jnp.min(jnp.where(mask & (lg == best), lane, float(LANES)), axis=-1, keepdims=True)
        return idx

    grp = top(lane < N_GROUPS)
    lo = N_GROUPS + EXPERTS_PER_GROUP * grp
    in_grp = (lane >= lo) & (lane < lo + EXPERTS_PER_GROUP)
    i1 = top(in_grp)
    i2 = top(in_grp & (lane != i1))
    la = jnp.minimum(i1, i2) - lo
    lb = jnp.maximum(i1, i2) - lo
    bucket = grp * 6.0 + la * (7.0 - la) * 0.5 + (lb - la - 1.0)

    onehot = lane == bucket
    strict = (lax.broadcasted_iota(jnp.int32, (TT, TT), 0) > lax.broadcasted_iota(jnp.int32, (TT, TT), 1))
    prefix = jnp.dot(jnp.where(strict, 1.0, 0.0).astype(BF16), jnp.where(onehot, 1.0, 0.0).astype(BF16),
                     preferred_element_type=F32)
    rank = jnp.sum(jnp.where(onehot, prefix + carry_ref[...], 0.0), axis=-1, keepdims=True)
    carry = carry_ref[...] + jnp.sum(jnp.where(onehot, 1.0, 0.0), axis=0, keepdims=True)
    carry_ref[...] = carry
    meta_ref[...] = jnp.where(lane == 0.0, bucket, jnp.where(lane == 1.0, rank, 0.0)).astype(jnp.int32)
    cnt_ref[...] = jnp.broadcast_to(carry, cnt_ref.shape).astype(jnp.int32)


def _router(x, g, wr, br):
    n = x.shape[0]
    return pl.pallas_call(
        _router_kernel,
        grid=(n // TT,),
        in_specs=[pl.BlockSpec((TT, D_MODEL), lambda i: (i, 0)),
                  pl.BlockSpec((1, D_MODEL), lambda i: (0, 0)),
                  pl.BlockSpec((D_MODEL, LANES), lambda i: (0, 0)),
                  pl.BlockSpec((1, LANES), lambda i: (0, 0))],
        out_specs=[pl.BlockSpec((TT, D_MODEL), lambda i: (i, 0)),
                   pl.BlockSpec((TT, LANES), lambda i: (i, 0)),
                   pl.BlockSpec((8, LANES), lambda i: (0, 0))],
        out_shape=[jax.ShapeDtypeStruct((n, D_MODEL), F32),
                   jax.ShapeDtypeStruct((n, LANES), jnp.int32),
                   jax.ShapeDtypeStruct((8, LANES), jnp.int32)],
        scratch_shapes=[pltpu.VMEM((1, LANES), F32)],
        compiler_params=_cparams(("arbitrary",)),
        name="router",
    )(x, g, wr, br)


def _row_copy(src_ref, src_row, dst_ref, dst_row, sem):
    return pltpu.make_async_copy(src_ref.at[pl.ds(src_row, 1)], dst_ref.at[pl.ds(dst_row, 1)], sem)


def _dispatch_kernel(d_ref, x_ref, xs_in_ref, xs_ref, sem):
    del xs_in_ref

    def start(r, c):
        _row_copy(x_ref, r, xs_ref, d_ref[0, 0, r], sem).start()
        return c

    def wait(r, c):
        _row_copy(x_ref, 0, xs_ref, 0, sem).wait()
        return c

    lax.fori_loop(0, TT, start, 0, unroll=8)
    lax.fori_loop(0, TT, wait, 0, unroll=8)


def _dispatch(dest, xn, xs_init):
    n = xn.shape[0]
    return pl.pallas_call(
        _dispatch_kernel,
        grid=(n // TT,),
        in_specs=[pl.BlockSpec((1, 1, TT), lambda i: (i, 0, 0), memory_space=pltpu.SMEM),
                  pl.BlockSpec((TT, D_MODEL), lambda i: (i, 0)),
                  pl.BlockSpec(memory_space=pl.ANY)],
        out_specs=pl.BlockSpec(memory_space=pl.ANY),
        out_shape=jax.ShapeDtypeStruct(xs_init.shape, F32),
        scratch_shapes=[pltpu.SemaphoreType.DMA(())],
        input_output_aliases={2: 0},
        compiler_params=_cparams(("arbitrary",)),
        name="dispatch",
    )(dest, xn, xs_init)


def _combine_kernel(final_norm, d_ref, x_ref, ys_ref, g_ref, o_ref, buf_ref, sem):
    def start(r, c):
        _row_copy(ys_ref, d_ref[0, 0, r], buf_ref, r, sem).start()
        return c

    def wait(r, c):
        _row_copy(ys_ref, 0, buf_ref, 0, sem).wait()
        return c

    lax.fori_loop(0, TT, start, 0, unroll=8)
    lax.fori_loop(0, TT, wait, 0, unroll=8)
    y = x_ref[...] + buf_ref[...]
    o_ref[...] = _rms(y, g_ref[...]) if final_norm else y


def _combine(dest, x, ys, g, final_norm):
    n = x.shape[0]
    return pl.pallas_call(
        functools.partial(_combine_kernel, final_norm),
        grid=(n // TT,),
        in_specs=[pl.BlockSpec((1, 1, TT), lambda i: (i, 0, 0), memory_space=pltpu.SMEM),
                  pl.BlockSpec((TT, D_MODEL), lambda i: (i, 0)),
                  pl.BlockSpec(memory_space=pl.ANY),
                  pl.BlockSpec((1, D_MODEL), lambda i: (0, 0))],
        out_specs=pl.BlockSpec((TT, D_MODEL), lambda i: (i, 0)),
        out_shape=jax.ShapeDtypeStruct((n, D_MODEL), F32),
        scratch_shapes=[pltpu.VMEM((TT, D_MODEL), F32), pltpu.SemaphoreType.DMA(())],
        compiler_params=_cparams(("arbitrary",)),
        name="combine",
    )(dest, x, ys, g)


def _expert_kernel(ta_ref, tb_ref, tg_ref, tv_ref, xs_ref, wr_ref, br_ref,
                   wgu_a_ref, wd_a_ref, wgu_b_ref, wd_b_ref, o_ref):
    t = pl.program_id(0)

    @pl.when(tv_ref[t] == 0)
    def _():
        o_ref[...] = jnp.zeros_like(o_ref)

    @pl.when(tv_ref[t] != 0)
    def _():
        xs = xs_ref[...]
        lg = _route_logits(xs, wr_ref, br_ref)
        lane = lax.broadcasted_iota(jnp.int32, (TM_E, LANES), 1)

        def pick(idx):
            return jnp.sum(jnp.where(lane == idx, lg, 0.0), axis=-1, keepdims=True)

        gmask = lane < N_GROUPS
        gmax = jnp.max(jnp.where(gmask, lg, -jnp.inf), axis=-1, keepdims=True)
        gsum = jnp.sum(jnp.where(gmask, jnp.exp(lg - gmax), 0.0), axis=-1, keepdims=True)
        g_w = jnp.exp(pick(tg_ref[t]) - gmax) / gsum
        l_a = pick(N_GROUPS + ta_ref[t])
        l_b = pick(N_GROUPS + tb_ref[t])
        m = jnp.maximum(l_a, l_b)
        e_a = jnp.exp(l_a - m)
        e_b = jnp.exp(l_b - m)
        xb = xs.astype(BF16)

        def expert(wgu_ref, wd_ref):
            gu = jnp.dot(xb, wgu_ref[...], preferred_element_type=F32)
            hid = jax.nn.silu(gu[:, :D_EXPERT]) * gu[:, D_EXPERT:]
            return jnp.dot(hid.astype(BF16), wd_ref[...], preferred_element_type=F32)

        o_ref[...] = (g_w * e_a / (e_a + e_b)) * expert(wgu_a_ref, wd_a_ref) \
            + (g_w * e_b / (e_a + e_b)) * expert(wgu_b_ref, wd_b_ref)


def _experts(tile_a, tile_b, tile_g, tile_v, xs, wr, br, wgu, wd):
    rows = xs.shape[0]
    grid_spec = pltpu.PrefetchScalarGridSpec(
        num_scalar_prefetch=4,
        grid=(rows // TM_E,),
        in_specs=[pl.BlockSpec((TM_E, D_MODEL), lambda t, a, b, g, v: (t, 0)),
                  pl.BlockSpec((D_MODEL, LANES), lambda t, a, b, g, v: (0, 0)),
                  pl.BlockSpec((1, LANES), lambda t, a, b, g, v: (0, 0)),
                  pl.BlockSpec((None, D_MODEL, 2 * D_EXPERT), lambda t, a, b, g, v: (a[t], 0, 0)),
                  pl.BlockSpec((None, D_EXPERT, D_MODEL), lambda t, a, b, g, v: (a[t], 0, 0)),
                  pl.BlockSpec((None, D_MODEL, 2 * D_EXPERT), lambda t, a, b, g, v: (b[t], 0, 0)),
                  pl.BlockSpec((None, D_EXPERT, D_MODEL), lambda t, a, b, g, v: (b[t], 0, 0))],
        out_specs=pl.BlockSpec((TM_E, D_MODEL), lambda t, a, b, g, v: (t, 0)),
    )
    return pl.pallas_call(
        _expert_kernel,
        grid_spec=grid_spec,
        out_shape=jax.ShapeDtypeStruct((rows, D_MODEL), F32),
        compiler_params=_cparams(("arbitrary",)),
        name="experts",
    )(tile_a, tile_b, tile_g, tile_v, xs, wr, br, wgu, wd, wgu, wd)


def _plan(meta, counts, n_rows):
    bucket = meta[:, 0]
    rank = meta[:, 1]
    cnt = counts[0, :N_BUCKETS]
    padded = ((cnt + TM_E - 1) // TM_E) * TM_E
    ends = jnp.cumsum(padded)
    dest = (ends - padded)[bucket] + rank
    starts = jnp.arange(n_rows // TM_E, dtype=jnp.int32) * TM_E
    valid = starts < ends[-1]
    tile_bucket = jnp.searchsorted(ends, jnp.minimum(starts, ends[-1] - 1), side="right").astype(jnp.int32)
    tile_bucket = jnp.minimum(tile_bucket, N_BUCKETS - 1)
    grp = tile_bucket // 6
    pair = tile_bucket % 6
    tile_a = grp * EXPERTS_PER_GROUP + jnp.asarray(PAIR_LO, jnp.int32)[pair]
    tile_b = grp * EXPERTS_PER_GROUP + jnp.asarray(PAIR_HI, jnp.int32)[pair]
    return dest.astype(jnp.int32), tile_a, tile_b, grp, valid.astype(jnp.int32)


def _block_diag(w):
    nb, d, _ = w.shape
    return jnp.einsum("hij,hg->higj", w, jnp.eye(nb, dtype=w.dtype)).reshape(nb * d, nb * d)


def kernel(x, norm1_g, w_in, b_gate, rel_bias, sgu_norm_g, sgu_w, sgu_b, conv_w, conv_b, rg_wa, rg_ba, rg_wx,
           rg_bx, rg_lambda, w_branch, w_out, norm2_g, router_grp_w, router_grp_b, router_exp_w, router_exp_b,
           exp_w_gate, exp_w_up, exp_w_down, final_norm_g):
    batch, seq, _ = x.shape
    depth = w_in.shape[0]
    n = batch * seq
    assert seq % TQ == 0 and seq % TM_MIX == 0 and n % TT == 0 and n % TM_IN == 0
    n_rows = n + N_BUCKETS * TM_E
    qkv_etc = 3 * A_WIDTH + 2 * B_WIDTH + 2 * C_WIDTH

    w_in_r = jnp.concatenate([w_in[:, :, qkv_etc:], w_in[:, :, :qkv_etc]], axis=2).astype(BF16)
    wgu = jnp.concatenate([exp_w_gate, exp_w_up], axis=-1).astype(BF16)
    wd = exp_w_down.astype(BF16)
    w_br = w_branch.astype(BF16)
    w_o = w_out.astype(BF16)
    pad = LANES - N_GROUPS - N_EXPERTS
    w_r = jnp.pad(jnp.concatenate([router_grp_w, router_exp_w], axis=-1), ((0, 0), (0, 0), (0, pad)))
    b_r = jnp.pad(jnp.concatenate([router_grp_b, router_exp_b], axis=-1), ((0, 0), (0, pad)))

    x2 = x.reshape(n, D_MODEL)
    xs_init = jnp.zeros((n_rows, D_MODEL), F32)
    for l in range(depth):
        p = {
            "b_gate": b_gate[l].reshape(1, GATE_COLS),
            "sgu_norm_g": sgu_norm_g[l].reshape(1, B_WIDTH),
            "sgu_w": sgu_w[l],
            "sgu_b": jnp.repeat(sgu_b[l].T, LANES, axis=1),
            "conv_w": conv_w[l],
            "conv_b": conv_b[l].reshape(1, C_WIDTH),
            "w_ax": jnp.concatenate([_block_diag(rg_wa[l]), _block_diag(rg_wx[l])], axis=1).astype(BF16),
            "b_ax": jnp.concatenate([rg_ba[l], rg_bx[l]]).reshape(1, 2 * C_WIDTH),
            "lam": rg_lambda[l].reshape(1, C_WIDTH),
            "w_branch": w_br[l],
            "w_out": w_o[l],
        }
        proj = _in_proj(x2, norm1_g[l].reshape(1, D_MODEL), w_in_r[l])
        att = _attention(proj, _attn_bias_table(rel_bias[l]), batch, seq)
        x2 = _mix(proj, att, x2, p, batch, seq)

        wr_l = w_r[l]
        br_l = b_r[l].reshape(1, LANES)
        xn, meta, counts = _router(x2, norm2_g[l].reshape(1, D_MODEL), wr_l, br_l)
        dest, tile_a, tile_b, tile_g, tile_v = _plan(meta, counts, n_rows)
        dest3 = dest.reshape(n // TT, 1, TT)
        xs = _dispatch(dest3, xn, xs_init)
        ys = _experts(tile_a, tile_b, tile_g, tile_v, xs, wr_l, br_l, wgu[l], wd[l])
        last = l == depth - 1
        x2 = _combine(dest3, x2, ys, final_norm_g.reshape(1, D_MODEL), last)
    return x2.reshape(batch, seq, D_MODEL)
```

```python
import jax
import jax.numpy as jnp
from jax import lax
from jax.experimental import pallas as pl
from jax.experimental.pallas import tpu as pltpu

F32 = jnp.float32
BF16 = jnp.bfloat16

D_MODEL = 1024
CHUNK = 64
A_HEADS = 8
A_HEAD_DIM = 64
A_WIDTH = 512
A_LEFT_CHUNKS = 8
REL_CLIP = 128
B_BLOCK = 128
B_GROUPS = 4
B_WIDTH = 512
C_WIDTH = 512
C_BLOCKS = 8
CONV_WIDTH = 4
LRU_C = 8.0
N_BRANCH = 3
GATE_COLS = N_BRANCH * D_MODEL
IN_COLS = 3 * A_WIDTH + 2 * B_WIDTH + 2 * C_WIDTH + GATE_COLS
N_GROUPS = 4
EXPERTS_PER_GROUP = 4
N_EXPERTS = 16
D_EXPERT = 512
EPS = 1e-6
NEG = -1e30

COL_BLK = 512
Q_BLK, K_BLK, V_BLK, U_BLK, SV_BLK, XG_BLK, XR_BLK = 6, 7, 8, 9, 10, 11, 12

TM_IN = 512
TQ = 256
TM_MIX = 256
TT = 512
TM_E = 256
N_BUCKETS = N_GROUPS * 6
LANES = 128
ROW_W = D_MODEL + LANES
VMEM_LIMIT = 56 * 1024 * 1024


def _cparams(sem):
    return pltpu.CompilerParams(dimension_semantics=sem, vmem_limit_bytes=VMEM_LIMIT)


def _rms(x, g):
    return x * lax.rsqrt(jnp.mean(x * x, axis=-1, keepdims=True) + EPS) * g


def _sigmoid(x):
    return 0.5 * jnp.tanh(0.5 * x) + 0.5


def _row_copy(src_ref, src_row, dst_ref, dst_row, sem):
    return pltpu.make_async_copy(src_ref.at[pl.ds(src_row, 1)], dst_ref.at[pl.ds(dst_row, 1)], sem)


def _project(x, g_ref, w_ref, o_ref, between=None):
    xn = _rms(x, g_ref[...]).astype(BF16)
    for j in range(IN_COLS // COL_BLK):
        sl = slice(j * COL_BLK, (j + 1) * COL_BLK)
        o_ref[:, sl] = jnp.dot(xn, w_ref[:, sl], preferred_element_type=F32).astype(BF16)
        if between is not None:
            between(j)


def _inproj_kernel(x_ref, g_ref, w_ref, o_ref):
    _project(x_ref[...], g_ref, w_ref, o_ref)


def _inproj_combine_kernel(dcur_ref, dnext_ref, x_ref, ys_ref, g_ref, w_ref, xo_ref, o_ref, buf_ref, sem):
    i = pl.program_id(0)
    slot = i % 2
    nxt = 1 - slot

    def gather(d_ref, s, r):
        return _row_copy(ys_ref, d_ref[0, 0, r], buf_ref.at[s], r, sem.at[s])

    def wait_rows(s):
        def body(r, c):
            _row_copy(ys_ref, 0, buf_ref.at[s], 0, sem.at[s]).wait()
            return c
        lax.fori_loop(0, TM_IN, body, 0, unroll=8)

    @pl.when(i == 0)
    def _():
        def body(r, c):
            gather(dcur_ref, 0, r).start()
            return c
        lax.fori_loop(0, TM_IN, body, 0, unroll=8)

    wait_rows(slot)
    x_new = x_ref[...] + buf_ref[slot]
    xo_ref[...] = x_new

    n_chunks = IN_COLS // COL_BLK
    per = -(-TM_IN // n_chunks)

    def issue(j):
        for r in range(j * per, min((j + 1) * per, TM_IN)):
            gather(dnext_ref, nxt, r).start()

    _project(x_new, g_ref, w_ref, o_ref, between=issue)

    @pl.when(i == pl.num_programs(0) - 1)
    def _():
        wait_rows(nxt)


def _in_proj(x, g, w, dest3=None, ys=None):
    n = x.shape[0]
    nt = n // TM_IN
    x_spec = pl.BlockSpec((TM_IN, D_MODEL), lambda i: (i, 0))
    g_spec = pl.BlockSpec((1, D_MODEL), lambda i: (0, 0))
    w_spec = pl.BlockSpec((D_MODEL, IN_COLS), lambda i: (0, 0), pipeline_mode=pl.Buffered(1))
    o_spec = pl.BlockSpec((TM_IN, IN_COLS), lambda i: (i, 0))
    o_shape = jax.ShapeDtypeStruct((n, IN_COLS), BF16)
    if ys is None:
        return x, pl.pallas_call(
            _inproj_kernel, grid=(nt,), in_specs=[x_spec, g_spec, w_spec], out_specs=o_spec, out_shape=o_shape,
            compiler_params=_cparams(("parallel",)), name="in_proj",
        )(x, g, w)
    assert TM_IN == TT
    return pl.pallas_call(
        _inproj_combine_kernel,
        grid=(nt,),
        in_specs=[pl.BlockSpec((1, 1, TT), lambda i: (i, 0, 0), memory_space=pltpu.SMEM),
                  pl.BlockSpec((1, 1, TT), lambda i: (jnp.minimum(i + 1, nt - 1), 0, 0), memory_space=pltpu.SMEM),
                  x_spec, pl.BlockSpec(memory_space=pl.ANY), g_spec, w_spec],
        out_specs=[x_spec, o_spec],
        out_shape=[jax.ShapeDtypeStruct((n, D_MODEL), F32), o_shape],
        scratch_shapes=[pltpu.VMEM((2, TM_IN, D_MODEL), F32), pltpu.SemaphoreType.DMA((2,))],
        compiler_params=_cparams(("arbitrary",)),
        name="in_proj_combine",
    )(dest3, dest3, x, ys, g, w)


def _attn_kernel(q_ref, k0_ref, k1_ref, k2_ref, v0_ref, v1_ref, v2_ref, bias_ref, o_ref):
    i = pl.program_id(1)
    col = lax.broadcasted_iota(jnp.int32, (TQ, 3 * TQ), 1)
    dead = col < jnp.maximum(2 - i, 0) * TQ
    lane = lax.broadcasted_iota(jnp.int32, (TQ, LANES), 1)
    first = lane < A_HEAD_DIM
    for p in range(A_HEADS // 2):
        sl = slice(p * LANES, (p + 1) * LANES)
        qp = q_ref[:, sl]
        kp = jnp.concatenate([k0_ref[:, sl], k1_ref[:, sl], k2_ref[:, sl]], axis=0)
        vp = jnp.concatenate([v0_ref[:, sl], v1_ref[:, sl], v2_ref[:, sl]], axis=0)
        outs = []
        for hh in range(2):
            qh = jnp.where(first if hh == 0 else jnp.logical_not(first), qp, jnp.zeros_like(qp))
            s = lax.dot_general(qh, kp, (((1,), (1,)), ((), ())), preferred_element_type=F32)
            s = s * (A_HEAD_DIM ** -0.5) + bias_ref[2 * p + hh]
            s = jnp.where(dead, NEG, s)
            e = jnp.exp(s - jnp.max(s, axis=-1, keepdims=True))
            l = jnp.sum(e, axis=-1, keepdims=True)
            outs.append(jnp.dot(e.astype(BF16), vp, preferred_element_type=F32) / l)
        o_ref[:, sl] = jnp.where(first, outs[0], outs[1]).astype(BF16)


def _attention(proj, bias, batch, seq):
    nq = seq // TQ

    def blk(col, back):
        return pl.BlockSpec((TQ, COL_BLK), lambda b, i: (b * nq + jnp.maximum(i - back, 0), col))

    return pl.pallas_call(
        _attn_kernel,
        grid=(batch, nq),
        in_specs=[blk(Q_BLK, 0), blk(K_BLK, 2), blk(K_BLK, 1), blk(K_BLK, 0),
                  blk(V_BLK, 2), blk(V_BLK, 1), blk(V_BLK, 0),
                  pl.BlockSpec((A_HEADS, TQ, 3 * TQ), lambda b, i: (0, 0, 0))],
        out_specs=pl.BlockSpec((TQ, A_WIDTH), lambda b, i: (b * nq + i, 0)),
        out_shape=jax.ShapeDtypeStruct((batch * seq, A_WIDTH), BF16),
        compiler_params=_cparams(("parallel", "parallel")),
        name="attention",
    )(proj, proj, proj, proj, proj, proj, proj, bias)


def _attn_bias_table(rel_bias):
    q = jnp.arange(TQ)
    k = jnp.arange(3 * TQ)
    rel = jnp.clip((2 * TQ + q)[:, None] - k[None, :], -REL_CLIP, REL_CLIP) + REL_CLIP
    qc = (q // CHUNK)[:, None]
    kc = (k // CHUNK)[None, :]
    band = (kc >= qc) & (kc <= qc + A_LEFT_CHUNKS)
    return jnp.where(band[None], rel_bias.astype(F32)[:, rel], NEG)


def _route(x_new, g2, wr_ref, br_ref, carry_ref):
    rows = x_new.shape[0]
    xn = _rms(x_new, g2)
    x_hi = xn.astype(BF16)
    x_lo = (xn - x_hi.astype(F32)).astype(BF16)
    p_hi = jnp.dot(x_hi, wr_ref[...], preferred_element_type=F32)
    p_lo = jnp.dot(x_lo, wr_ref[...], preferred_element_type=F32)
    lg = p_hi + pltpu.roll(p_hi, LANES - 32, 1) + p_lo + br_ref[...]
    lane = lax.broadcasted_iota(jnp.int32, (rows, LANES), 1).astype(F32)

    def top(mask):
        best = jnp.max(jnp.where(mask, lg, -jnp.inf), axis=-1, keepdims=True)
        idx = jnp.min(jnp.where(mask & (lg == best), lane, float(LANES)), axis=-1, keepdims=True)
        return best, idx

    gmask = lane < N_GROUPS
    gmax, grp = top(gmask)
    g_w = 1.0 / jnp.sum(jnp.where(gmask, jnp.exp(lg - gmax), 0.0), axis=-1, keepdims=True)
    lo = N_GROUPS + EXPERTS_PER_GROUP * grp
    in_grp = (lane >= lo) & (lane < lo + EXPERTS_PER_GROUP)
    l1, i1 = top(in_grp)
    l2, i2 = top(in_grp & (lane != i1))
    e2 = jnp.exp(l2 - l1)
    w1 = g_w / (1.0 + e2)
    w2 = g_w * e2 / (1.0 + e2)
    first_is_low = i1 < i2
    cw_a = jnp.where(first_is_low, w1, w2)
    cw_b = jnp.where(first_is_low, w2, w1)
    la = jnp.minimum(i1, i2) - lo
    lb = jnp.maximum(i1, i2) - lo
    bucket = grp * 6.0 + la * (7.0 - la) * 0.5 + (lb - la - 1.0)

    onehot = lane == bucket
    strict = (lax.broadcasted_iota(jnp.int32, (rows, rows), 0) > lax.broadcasted_iota(jnp.int32, (rows, rows), 1))
    prefix = jnp.dot(jnp.where(strict, 1.0, 0.0).astype(BF16), jnp.where(onehot, 1.0, 0.0).astype(BF16),
                     preferred_element_type=F32)
    carry = carry_ref[...]
    rank = jnp.sum(jnp.where(onehot, prefix + carry, 0.0), axis=-1, keepdims=True)
    carry = carry + jnp.sum(jnp.where(onehot, 1.0, 0.0), axis=0, keepdims=True)
    meta = jnp.where(lane == 0.0, bucket, jnp.where(lane == 1.0, rank, jnp.where(lane == 2.0, cw_a,
                     jnp.where(lane == 3.0, cw_b, 0.0))))
    return meta, carry


def _scan_linear(a, b):
    n = a.shape[0]
    rows = lax.broadcasted_iota(jnp.int32, a.shape, 0)
    s = 1
    while s < n:
        keep = rows >= s
        b = a * jnp.where(keep, pltpu.roll(b, s, 0), 0.0) + b
        a = a * jnp.where(keep, pltpu.roll(a, s, 0), 1.0)
        s *= 2
    return a, b


def _mix_kernel(gl_ref, u_ref, sv_ref, xg_ref, xr_ref, att_ref, x_ref,
                bgate_ref, sgn_ref, sgw_ref, sgb_ref, cw_ref, cb_ref, wax_ref, bax_ref, lam_ref,
                wbr_ref, wout_ref, g2_ref, wr_ref, br_ref,
                o_ref, meta_ref, cnt_ref, ext_ref, h_ref, carry_ref):
    t = pl.program_id(1)

    @pl.when((pl.program_id(0) == 0) & (t == 0))
    def _():
        carry_ref[...] = jnp.zeros_like(carry_ref)

    @pl.when(t == 0)
    def _():
        ext_ref[0:8, :] = jnp.zeros((8, C_WIDTH), F32)
        h_ref[...] = jnp.zeros_like(h_ref)

    u = jax.nn.gelu(u_ref[...].astype(F32))
    v = _rms(jax.nn.gelu(sv_ref[...].astype(F32)), sgn_ref[...]).astype(BF16)
    tri = (lax.broadcasted_iota(jnp.int32, (B_BLOCK, B_BLOCK), 0)
           >= lax.broadcasted_iota(jnp.int32, (B_BLOCK, B_BLOCK), 1))
    blocks = []
    for blk in range(TM_MIX // B_BLOCK):
        rs = slice(blk * B_BLOCK, (blk + 1) * B_BLOCK)
        groups = []
        for g in range(B_GROUPS):
            cs = slice(g * LANES, (g + 1) * LANES)
            w = jnp.where(tri, sgw_ref[g], 0.0).astype(BF16)
            groups.append(jnp.dot(w, v[rs, cs], preferred_element_type=F32))
        blocks.append(jnp.concatenate(groups, axis=1) + sgb_ref[...])
    b_out = (u * jnp.concatenate(blocks, axis=0)).astype(BF16)

    xr = xr_ref[...].astype(F32)
    ext_ref[8:, :] = xr
    xc = cb_ref[...]
    for j in range(CONV_WIDTH):
        lo = 8 - (CONV_WIDTH - 1) + j
        xc = xc + cw_ref[j:j + 1, :] * ext_ref[lo:lo + TM_MIX, :]
    ext_ref[0:8, :] = xr[TM_MIX - 8:, :]
    ri = jnp.dot(xc.astype(BF16), wax_ref[...], preferred_element_type=F32) + bax_ref[...]
    r = _sigmoid(ri[:, :C_WIDTH])
    ig = _sigmoid(ri[:, C_WIDTH:])
    z = -lam_ref[...]
    softplus = jnp.maximum(z, 0.0) + jnp.log1p(jnp.exp(-jnp.abs(z)))
    log_a = -LRU_C * r * softplus
    a = jnp.exp(log_a)
    th = jnp.tanh(log_a)
    mult = jnp.sqrt(jnp.maximum(-2.0 * th / (1.0 - th), 0.0))
    a_cum, h = _scan_linear(a, mult * (ig * xc))
    h = a_cum * h_ref[...] + h
    h_ref[...] = h[TM_MIX - 1:, :]
    c_out = (jax.nn.gelu(xg_ref[...].astype(F32)) * h).astype(BF16)

    merged = None
    for k, br in enumerate((att_ref[...], b_out, c_out)):
        cs = slice(k * D_MODEL, (k + 1) * D_MODEL)
        gate = _sigmoid(gl_ref[:, cs].astype(F32) + bgate_ref[:, cs])
        term = gate * jnp.dot(br, wbr_ref[k], preferred_element_type=F32)
        merged = term if merged is None else merged + term
    x_new = x_ref[...] + jnp.dot(merged.astype(BF16), wout_ref[...], preferred_element_type=F32)
    o_ref[...] = x_new

    meta, carry = _route(x_new, g2_ref[...], wr_ref, br_ref, carry_ref)
    carry_ref[...] = carry
    meta_ref[...] = meta
    cnt_ref[...] = jnp.broadcast_to(carry, cnt_ref.shape)


def _mix(proj, att, x, p, batch, seq):
    nt = seq // TM_MIX
    n = batch * seq

    def row(width, col):
        return pl.BlockSpec((TM_MIX, width), lambda b, t: (b * nt + t, col))

    def const(shape):
        return pl.BlockSpec(shape, lambda b, t: (0,) * len(shape))

    return pl.pallas_call(
        _mix_kernel,
        grid=(batch, nt),
        in_specs=[row(GATE_COLS, 0), row(COL_BLK, U_BLK), row(COL_BLK, SV_BLK), row(COL_BLK, XG_BLK),
                  row(COL_BLK, XR_BLK), row(A_WIDTH, 0), row(D_MODEL, 0),
                  const((1, GATE_COLS)), const((1, B_WIDTH)), const((B_GROUPS, B_BLOCK, B_BLOCK)),
                  const((B_BLOCK, B_WIDTH)), const((CONV_WIDTH, C_WIDTH)), const((1, C_WIDTH)),
                  const((C_WIDTH, 2 * C_WIDTH)), const((1, 2 * C_WIDTH)), const((1, C_WIDTH)),
                  const((N_BRANCH, A_WIDTH, D_MODEL)), const((D_MODEL, D_MODEL)),
                  const((1, D_MODEL)), const((D_MODEL, LANES)), const((1, LANES))],
        out_specs=[row(D_MODEL, 0), row(LANES, 0), const((8, LANES))],
        out_shape=[jax.ShapeDtypeStruct((n, D_MODEL), F32),
                   jax.ShapeDtypeStruct((n, LANES), F32),
                   jax.ShapeDtypeStruct((8, LANES), F32)],
        scratch_shapes=[pltpu.VMEM((TM_MIX + 8, C_WIDTH), F32), pltpu.VMEM((1, C_WIDTH), F32),
                        pltpu.VMEM((1, LANES), F32)],
        compiler_params=_cparams(("arbitrary", "arbitrary")),
        name="mix",
    )(proj, proj, proj, proj, proj, att, x,
      p["b_gate"], p["sgu_norm_g"], p["sgu_w"], p["sgu_b"], p["conv_w"], p["conv_b"],
      p["w_ax"], p["b_ax"], p["lam"], p["w_branch"], p["w_out"], p["norm2_g"], p["w_r"], p["b_r"])


def _dispatch_kernel(d_ref, x_ref, g_ref, meta_ref, xs_in_ref, xs_ref, rows_ref, sem):
    del xs_in_ref
    rows_ref[:, :D_MODEL] = _rms(x_ref[...], g_ref[...])
    rows_ref[:, D_MODEL:] = meta_ref[...]

    def start(r, c):
        _row_copy(rows_ref, r, xs_ref, d_ref[0, 0, r], sem).start()
        return c

    def wait(r, c):
        _row_copy(rows_ref, 0, xs_ref, 0, sem).wait()
        return c

    lax.fori_loop(0, TT, start, 0, unroll=8)
    lax.fori_loop(0, TT, wait, 0, unroll=8)


def _dispatch(dest, x, g, meta, xs_init):
    n = x.shape[0]
    return pl.pallas_call(
        _dispatch_kernel,
        grid=(n // TT,),
        in_specs=[pl.BlockSpec((1, 1, TT), lambda i: (i, 0, 0), memory_space=pltpu.SMEM),
                  pl.BlockSpec((TT, D_MODEL), lambda i: (i, 0)),
                  pl.BlockSpec((1, D_MODEL), lambda i: (0, 0)),
                  pl.BlockSpec((TT, LANES), lambda i: (i, 0)),
                  pl.BlockSpec(memory_space=pl.ANY)],
        out_specs=pl.BlockSpec(memory_space=pl.ANY),
        out_shape=jax.ShapeDtypeStruct(xs_init.shape, F32),
        scratch_shapes=[pltpu.VMEM((TT, ROW_W), F32), pltpu.SemaphoreType.DMA(())],
        input_output_aliases={4: 0},
        compiler_params=_cparams(("arbitrary",)),
        name="dispatch",
    )(dest, x, g, meta, xs_init)


def _combine_kernel(d_ref, x_ref, ys_ref, g_ref, o_ref, buf_ref, sem):
    def start(r, c):
        _row_copy(ys_ref, d_ref[0, 0, r], buf_ref, r, sem).start()
        return c

    def wait(r, c):
        _row_copy(ys_ref, 0, buf_ref, 0, sem).wait()
        return c

    lax.fori_loop(0, TT, start, 0, unroll=8)
    lax.fori_loop(0, TT, wait, 0, unroll=8)
    o_ref[...] = _rms(x_ref[...] + buf_ref[...], g_ref[...])


def _combine(dest, x, ys, g):
    n = x.shape[0]
    return pl.pallas_call(
        _combine_kernel,
        grid=(n // TT,),
        in_specs=[pl.BlockSpec((1, 1, TT), lambda i: (i, 0, 0), memory_space=pltpu.SMEM),
                  pl.BlockSpec((TT, D_MODEL), lambda i: (i, 0)),
                  pl.BlockSpec(memory_space=pl.ANY),
                  pl.BlockSpec((1, D_MODEL), lambda i: (0, 0))],
        out_specs=pl.BlockSpec((TT, D_MODEL), lambda i: (i, 0)),
        out_shape=jax.ShapeDtypeStruct((n, D_MODEL), F32),
        scratch_shapes=[pltpu.VMEM((TT, D_MODEL), F32), pltpu.SemaphoreType.DMA(())],
        compiler_params=_cparams(("arbitrary",)),
        name="combine",
    )(dest, x, ys, g)


def _expert_kernel(ta_ref, tb_ref, tv_ref, xs_ref, wgu_a_ref, wd_a_ref, wgu_b_ref, wd_b_ref, o_ref):
    del ta_ref, tb_ref
    t = pl.program_id(0)

    @pl.when(tv_ref[t] == 0)
    def _():
        o_ref[...] = jnp.zeros_like(o_ref)

    @pl.when(tv_ref[t] != 0)
    def _():
        xb = xs_ref[:, :D_MODEL].astype(BF16)
        cw = xs_ref[:, D_MODEL:]

        def expert(wgu_ref, wd_ref):
            gu = jnp.dot(xb, wgu_ref[...], preferred_element_type=F32)
            hid = gu[:, :D_EXPERT] * _sigmoid(gu[:, :D_EXPERT]) * gu[:, D_EXPERT:]
            return jnp.dot(hid.astype(BF16), wd_ref[...], preferred_element_type=F32)

        o_ref[...] = cw[:, 2:3] * expert(wgu_a_ref, wd_a_ref) + cw[:, 3:4] * expert(wgu_b_ref, wd_b_ref)


def _experts(tile_a, tile_b, tile_v, xs, wgu, wd):
    rows = xs.shape[0]
    grid_spec = pltpu.PrefetchScalarGridSpec(
        num_scalar_prefetch=3,
        grid=(rows // TM_E,),
        in_specs=[pl.BlockSpec((TM_E, ROW_W), lambda t, a, b, v: (t, 0)),
                  pl.BlockSpec((None, D_MODEL, 2 * D_EXPERT), lambda t, a, b, v: (a[t], 0, 0)),
                  pl.BlockSpec((None, D_EXPERT, D_MODEL), lambda t, a, b, v: (a[t], 0, 0)),
                  pl.BlockSpec((None, D_MODEL, 2 * D_EXPERT), lambda t, a, b, v: (b[t], 0, 0)),
                  pl.BlockSpec((None, D_EXPERT, D_MODEL), lambda t, a, b, v: (b[t], 0, 0))],
        out_specs=pl.BlockSpec((TM_E, D_MODEL), lambda t, a, b, v: (t, 0)),
    )
    return pl.pallas_call(
        _expert_kernel,
        grid_spec=grid_spec,
        out_shape=jax.ShapeDtypeStruct((rows, D_MODEL), F32),
        compiler_params=_cparams(("arbitrary",)),
        name="experts",
    )(tile_a, tile_b, tile_v, xs, wgu, wd, wgu, wd)


def _plan(meta, counts, n_rows):
    bucket = meta[:, 0].astype(jnp.int32)
    rank = meta[:, 1].astype(jnp.int32)
    cnt = counts[0, :N_BUCKETS].astype(jnp.int32)
    padded = ((cnt + TM_E - 1) // TM_E) * TM_E
    ends = jnp.cumsum(padded)
    base = ends - padded
    ids = jnp.arange(N_BUCKETS, dtype=jnp.int32)
    dest = jnp.sum(jnp.where(bucket[:, None] == ids[None, :], base[None, :], 0), axis=1) + rank
    starts = jnp.arange(n_rows // TM_E, dtype=jnp.int32) * TM_E
    valid = starts < ends[-1]
    probe = jnp.minimum(starts, ends[-1] - 1)
    tile_bucket = jnp.minimum(jnp.sum((probe[:, None] >= ends[None, :]).astype(jnp.int32), axis=1), N_BUCKETS - 1)
    grp = tile_bucket // 6
    pair = tile_bucket % 6
    ge3 = (pair >= 3).astype(jnp.int32)
    ge5 = (pair >= 5).astype(jnp.int32)
    tile_a = grp * EXPERTS_PER_GROUP + ge3 + ge5
    tile_b = grp * EXPERTS_PER_GROUP + pair + 1 - 2 * ge3 - ge5
    return dest, tile_a, tile_b, valid.astype(jnp.int32)


def _block_diag(w):
    nb, d, _ = w.shape
    return jnp.einsum("hij,hg->higj", w, jnp.eye(nb, dtype=w.dtype)).reshape(nb * d, nb * d)


def kernel(x, norm1_g, w_in, b_gate, rel_bias, sgu_norm_g, sgu_w, sgu_b, conv_w, conv_b, rg_wa, rg_ba, rg_wx,
           rg_bx, rg_lambda, w_branch, w_out, norm2_g, router_grp_w, router_grp_b, router_exp_w, router_exp_b,
           exp_w_gate, exp_w_up, exp_w_down, final_norm_g):
    batch, seq, _ = x.shape
    depth = w_in.shape[0]
    n = batch * seq
    assert seq % TQ == 0 and seq % TM_MIX == 0 and n % TT == 0 and n % TM_IN == 0
    n_rows = n + N_BUCKETS * TM_E
    qkv_etc = 3 * A_WIDTH + 2 * B_WIDTH + 2 * C_WIDTH

    w_in_r = jnp.concatenate([w_in[:, :, qkv_etc:], w_in[:, :, :qkv_etc]], axis=2).astype(BF16)
    wgu = jnp.concatenate([exp_w_gate, exp_w_up], axis=-1).astype(BF16)
    wd = exp_w_down.astype(BF16)
    w_br = w_branch.astype(BF16)
    w_o = w_out.astype(BF16)
    n_logits = N_GROUPS + N_EXPERTS
    w_r32 = jnp.concatenate([router_grp_w, router_exp_w], axis=-1)
    w_r_hi = w_r32.astype(BF16)
    w_r_lo = (w_r32 - w_r_hi.astype(F32)).astype(BF16)
    zpad = lambda w, k: jnp.pad(w, ((0, 0), (0, 0), (0, k)))
    w_r = jnp.concatenate([zpad(w_r_hi, 32 - n_logits), zpad(w_r_lo, LANES - 32 - n_logits)], axis=-1)
    b_r = jnp.pad(jnp.concatenate([router_grp_b, router_exp_b], axis=-1), ((0, 0), (0, LANES - n_logits)))

    x2 = x.reshape(n, D_MODEL)
    xs_init = jnp.zeros((n_rows, ROW_W), F32)
    dest3 = ys = None
    for l in range(depth):
        p = {
            "b_gate": b_gate[l].reshape(1, GATE_COLS),
            "sgu_norm_g": sgu_norm_g[l].reshape(1, B_WIDTH),
            "sgu_w": sgu_w[l],
            "sgu_b": jnp.repeat(sgu_b[l].T, LANES, axis=1),
            "conv_w": conv_w[l],
            "conv_b": conv_b[l].reshape(1, C_WIDTH),
            "w_ax": jnp.concatenate([_block_diag(rg_wa[l]), _block_diag(rg_wx[l])], axis=1).astype(BF16),
            "b_ax": jnp.concatenate([rg_ba[l], rg_bx[l]]).reshape(1, 2 * C_WIDTH),
            "lam": rg_lambda[l].reshape(1, C_WIDTH),
            "w_branch": w_br[l],
            "w_out": w_o[l],
            "norm2_g": norm2_g[l].reshape(1, D_MODEL),
            "w_r": w_r[l],
            "b_r": b_r[l].reshape(1, LANES),
        }
        x2, proj = _in_proj(x2, norm1_g[l].reshape(1, D_MODEL), w_in_r[l], dest3, ys)
        att = _attention(proj, _attn_bias_table(rel_bias[l]), batch, seq)
        x2, meta, counts = _mix(proj, att, x2, p, batch, seq)

        dest, tile_a, tile_b, tile_v = _plan(meta, counts, n_rows)
        dest3 = dest.reshape(n // TT, 1, TT)
        xs = _dispatch(dest3, x2, p["norm2_g"], meta, xs_init)
        ys = _experts(tile_a, tile_b, tile_v, xs, wgu[l], wd[l])
    x2 = _combine(dest3, x2, ys, final_norm_g.reshape(1, D_MODEL))
    return x2.reshape(batch, seq, D_MODEL)
```

```python
import jax
import jax.numpy as jnp
from jax import lax
from jax.experimental import pallas as pl
from jax.experimental.pallas import tpu as pltpu

F32 = jnp.float32
BF16 = jnp.bfloat16

D_MODEL = 1024
CHUNK = 64
A_HEADS = 8
A_HEAD_DIM = 64
A_WIDTH = 512
A_LEFT_CHUNKS = 8
REL_CLIP = 128
B_BLOCK = 128
B_GROUPS = 4
B_WIDTH = 512
C_WIDTH = 512
C_BLOCKS = 8
CONV_WIDTH = 4
LRU_C = 8.0
N_BRANCH = 3
GATE_COLS = N_BRANCH * D_MODEL
IN_COLS = 3 * A_WIDTH + 2 * B_WIDTH + 2 * C_WIDTH + GATE_COLS
N_GROUPS = 4
EXPERTS_PER_GROUP = 4
N_EXPERTS = 16
D_EXPERT = 512
EPS = 1e-6
NEG = -1e30
LOG2E = 1.4426950408889634

COL_BLK = 512
Q_BLK, K_BLK, V_BLK, U_BLK, SV_BLK, XG_BLK, XR_BLK = 6, 7, 8, 9, 10, 11, 12

TM_IN = 512
TQ = 256
TM_MIX = 256
TT = 512
TM_E = 256
N_BUCKETS = N_GROUPS * 6
LANES = 128
ROW_W = D_MODEL + LANES
VMEM_LIMIT = 56 * 1024 * 1024


def _cparams(sem):
    return pltpu.CompilerParams(dimension_semantics=sem, vmem_limit_bytes=VMEM_LIMIT)


def _rms(x, g):
    return x * lax.rsqrt(jnp.mean(x * x, axis=-1, keepdims=True) + EPS) * g


def _sigmoid(x):
    return 0.5 * jnp.tanh(0.5 * x) + 0.5


def _row_copy(src_ref, src_row, dst_ref, dst_row, sem):
    return pltpu.make_async_copy(src_ref.at[pl.ds(src_row, 1)], dst_ref.at[pl.ds(dst_row, 1)], sem)


def _gelu(x):
    c = 0.7978845608028654
    h = 0.5 * x
    return h + h * jnp.tanh(x * (c + (c * 0.044715) * (x * x)))


def _project(x, g_ref, bg_ref, w_ref, o_ref, between=None):
    xn = _rms(x, g_ref[...]).astype(BF16)
    for j in range(IN_COLS // COL_BLK):
        sl = slice(j * COL_BLK, (j + 1) * COL_BLK)
        acc = jnp.dot(xn, w_ref[:, sl], preferred_element_type=F32)
        if j < GATE_COLS // COL_BLK:
            acc = 1.0 + jnp.tanh(acc + bg_ref[:, sl])
        elif j in (U_BLK, SV_BLK, XG_BLK):
            acc = _gelu(acc)
        o_ref[:, sl] = acc.astype(BF16)
        if between is not None:
            between(j)


def _inproj_kernel(x_ref, g_ref, bg_ref, w_ref, o_ref):
    _project(x_ref[...], g_ref, bg_ref, w_ref, o_ref)


def _inproj_combine_kernel(dcur_ref, dnext_ref, x_ref, ys_ref, g_ref, bg_ref, w_ref, xo_ref, o_ref, buf_ref, sem):
    i = pl.program_id(0)
    slot = i % 2
    nxt = 1 - slot

    def gather(d_ref, s, r):
        return _row_copy(ys_ref, d_ref[0, 0, r], buf_ref.at[s], r, sem.at[s])

    def wait_rows(s):
        def body(r, c):
            _row_copy(ys_ref, 0, buf_ref.at[s], 0, sem.at[s]).wait()
            return c
        lax.fori_loop(0, TM_IN, body, 0, unroll=8)

    @pl.when(i == 0)
    def _():
        def body(r, c):
            gather(dcur_ref, 0, r).start()
            return c
        lax.fori_loop(0, TM_IN, body, 0, unroll=8)

    wait_rows(slot)
    x_new = x_ref[...] + buf_ref[slot]
    xo_ref[...] = x_new

    n_chunks = IN_COLS // COL_BLK
    per = -(-TM_IN // n_chunks)

    def issue(j):
        for r in range(j * per, min((j + 1) * per, TM_IN)):
            gather(dnext_ref, nxt, r).start()

    _project(x_new, g_ref, bg_ref, w_ref, o_ref, between=issue)

    @pl.when(i == pl.num_programs(0) - 1)
    def _():
        wait_rows(nxt)


def _in_proj(x, g, bg, w, dest3=None, ys=None):
    n = x.shape[0]
    nt = n // TM_IN
    x_spec = pl.BlockSpec((TM_IN, D_MODEL), lambda i: (i, 0))
    g_spec = pl.BlockSpec((1, D_MODEL), lambda i: (0, 0))
    bg_spec = pl.BlockSpec((1, GATE_COLS), lambda i: (0, 0))
    w_spec = pl.BlockSpec((D_MODEL, IN_COLS), lambda i: (0, 0), pipeline_mode=pl.Buffered(1))
    o_spec = pl.BlockSpec((TM_IN, IN_COLS), lambda i: (i, 0))
    o_shape = jax.ShapeDtypeStruct((n, IN_COLS), BF16)
    if ys is None:
        return x, pl.pallas_call(
            _inproj_kernel, grid=(nt,), in_specs=[x_spec, g_spec, bg_spec, w_spec], out_specs=o_spec,
            out_shape=o_shape, compiler_params=_cparams(("parallel",)), name="in_proj",
        )(x, g, bg, w)
    assert TM_IN == TT
    return pl.pallas_call(
        _inproj_combine_kernel,
        grid=(nt,),
        in_specs=[pl.BlockSpec((1, 1, TT), lambda i: (i, 0, 0), memory_space=pltpu.SMEM),
                  pl.BlockSpec((1, 1, TT), lambda i: (jnp.minimum(i + 1, nt - 1), 0, 0), memory_space=pltpu.SMEM),
                  x_spec, pl.BlockSpec(memory_space=pl.ANY), g_spec, bg_spec, w_spec],
        out_specs=[x_spec, o_spec],
        out_shape=[jax.ShapeDtypeStruct((n, D_MODEL), F32), o_shape],
        scratch_shapes=[pltpu.VMEM((2, TM_IN, D_MODEL), F32), pltpu.SemaphoreType.DMA((2,))],
        compiler_params=_cparams(("arbitrary",)),
        name="in_proj_combine",
    )(dest3, dest3, x, ys, g, bg, w)


def _attn_kernel(q_ref, k0_ref, k1_ref, k2_ref, v0_ref, v1_ref, v2_ref, bias_ref, o_ref):
    lane = lax.broadcasted_iota(jnp.int32, (TQ, LANES), 1)
    first = lane < A_HEAD_DIM
    for p in range(A_HEADS // 2):
        sl = slice(p * LANES, (p + 1) * LANES)
        qp = q_ref[:, sl]
        kp = jnp.concatenate([k0_ref[:, sl], k1_ref[:, sl], k2_ref[:, sl]], axis=0)
        vp = jnp.concatenate([v0_ref[:, sl], v1_ref[:, sl], v2_ref[:, sl]], axis=0)
        outs = []
        for hh in range(2):
            qh = jnp.where(first if hh == 0 else jnp.logical_not(first), qp, jnp.zeros_like(qp))
            s = lax.dot_general(qh, kp, (((1,), (1,)), ((), ())), preferred_element_type=F32)
            s = s + bias_ref[2 * p + hh]
            e = jnp.exp2(s - jnp.max(s, axis=-1, keepdims=True))
            l = jnp.sum(e, axis=-1, keepdims=True)
            outs.append(jnp.dot(e.astype(BF16), vp, preferred_element_type=F32) / l)
        o_ref[:, sl] = jnp.where(first, outs[0], outs[1]).astype(BF16)


def _attention(proj, bias, batch, seq):
    nq = seq // TQ

    def blk(col, back):
        return pl.BlockSpec((TQ, COL_BLK), lambda b, i: (b * nq + jnp.maximum(i - back, 0), col))

    return pl.pallas_call(
        _attn_kernel,
        grid=(batch, nq),
        in_specs=[blk(Q_BLK, 0), blk(K_BLK, 2), blk(K_BLK, 1), blk(K_BLK, 0),
                  blk(V_BLK, 2), blk(V_BLK, 1), blk(V_BLK, 0),
                  pl.BlockSpec((None, A_HEADS, TQ, 3 * TQ), lambda b, i: (jnp.minimum(i, 2), 0, 0, 0))],
        out_specs=pl.BlockSpec((TQ, A_WIDTH), lambda b, i: (b * nq + i, 0)),
        out_shape=jax.ShapeDtypeStruct((batch * seq, A_WIDTH), BF16),
        compiler_params=_cparams(("parallel", "parallel")),
        name="attention",
    )(proj, proj, proj, proj, proj, proj, proj, bias)


def _attn_bias_table(rel_bias):
    heads = rel_bias.shape[0]
    rb = rel_bias.astype(F32) * LOG2E
    width = 4 * TQ
    n_head = 3 * TQ - 1 - REL_CLIP
    n_tail = width - n_head - (2 * REL_CLIP + 1)
    diag = jnp.concatenate([jnp.broadcast_to(rb[:, -1:], (heads, n_head)), rb[:, ::-1],
                            jnp.broadcast_to(rb[:, :1], (heads, n_tail))], axis=1)
    rolled = jnp.roll(diag, -(TQ - 1), axis=1)
    flat = jnp.broadcast_to(rolled[:, None, :], (heads, TQ, width)).reshape(heads, TQ * width)
    toe = flat[:, :TQ * (width - 1)].reshape(heads, TQ, width - 1)[:, :, :3 * TQ]
    q = jnp.arange(TQ)
    k = jnp.arange(3 * TQ)
    qc = (q // CHUNK)[:, None]
    kc = (k // CHUNK)[None, :]
    band = (kc >= qc) & (kc <= qc + A_LEFT_CHUNKS)
    variants = []
    for v in range(3):
        live = band & (k[None, :] >= (2 - v) * TQ)
        variants.append(jnp.where(live[None], toe, NEG))
    return jnp.stack(variants)


def _route(x_new, g2, wr_ref, br_ref, carry_ref):
    rows = x_new.shape[0]
    xn = _rms(x_new, g2)
    x_hi = xn.astype(BF16)
    x_lo = (xn - x_hi.astype(F32)).astype(BF16)
    p_hi = jnp.dot(x_hi, wr_ref[...], preferred_element_type=F32)
    p_lo = jnp.dot(x_lo, wr_ref[...], preferred_element_type=F32)
    lg = p_hi + pltpu.roll(p_hi, LANES - 32, 1) + p_lo + br_ref[...]
    lane = lax.broadcasted_iota(jnp.int32, (rows, LANES), 1).astype(F32)

    def top(mask):
        best = jnp.max(jnp.where(mask, lg, -jnp.inf), axis=-1, keepdims=True)
        idx = jnp.min(jnp.where(mask & (lg == best), lane, float(LANES)), axis=-1, keepdims=True)
        return best, idx

    gmask = lane < N_GROUPS
    gmax, grp = top(gmask)
    g_w = 1.0 / jnp.sum(jnp.where(gmask, jnp.exp(lg - gmax), 0.0), axis=-1, keepdims=True)
    lo = N_GROUPS + EXPERTS_PER_GROUP * grp
    in_grp = (lane >= lo) & (lane < lo + EXPERTS_PER_GROUP)
    l1, i1 = top(in_grp)
    l2, i2 = top(in_grp & (lane != i1))
    e2 = jnp.exp(l2 - l1)
    w1 = g_w / (1.0 + e2)
    w2 = g_w * e2 / (1.0 + e2)
    first_is_low = i1 < i2
    cw_a = jnp.where(first_is_low, w1, w2)
    cw_b = jnp.where(first_is_low, w2, w1)
    la = jnp.minimum(i1, i2) - lo
    lb = jnp.maximum(i1, i2) - lo
    bucket = grp * 6.0 + la * (7.0 - la) * 0.5 + (lb - la - 1.0)

    onehot = lane == bucket
    strict = (lax.broadcasted_iota(jnp.int32, (rows, rows), 0) > lax.broadcasted_iota(jnp.int32, (rows, rows), 1))
    prefix = jnp.dot(jnp.where(strict, 1.0, 0.0).astype(BF16), jnp.where(onehot, 1.0, 0.0).astype(BF16),
                     preferred_element_type=F32)
    carry = carry_ref[...]
    rank = jnp.sum(jnp.where(onehot, prefix + carry, 0.0), axis=-1, keepdims=True)
    carry = carry + jnp.sum(jnp.where(onehot, 1.0, 0.0), axis=0, keepdims=True)
    meta = jnp.where(lane == 0.0, bucket, jnp.where(lane == 1.0, rank, jnp.where(lane == 2.0, cw_a,
                     jnp.where(lane == 3.0, cw_b, 0.0))))
    return meta, carry


def _scan_linear(a, b, h_prev):
    n, c = a.shape
    groups = n // 8
    a = a.reshape(groups, 8, c)
    b = b.reshape(groups, 8, c)
    sub = lax.broadcasted_iota(jnp.int32, a.shape, 1)
    for s in (1, 2, 4):
        keep = sub >= s
        b = a * jnp.where(keep, pltpu.roll(b, s, 1), 0.0) + b
        a = a * jnp.where(keep, pltpu.roll(a, s, 1), 1.0)
    outs = []
    for g in range(groups):
        hg = a[g] * h_prev + b[g]
        outs.append(hg)
        h_prev = hg[7:8, :]
    return jnp.concatenate(outs, axis=0), h_prev


def _mix_kernel(gl_ref, u_ref, sv_ref, xg_ref, xr_ref, att_ref, x_ref,
                sgn_ref, sgw_ref, sgb_ref, cw_ref, cb_ref, wax_ref, bax_ref, lam_ref,
                wbr_ref, wout_ref, g2_ref, wr_ref, br_ref,
                o_ref, meta_ref, cnt_ref, ext_ref, h_ref, carry_ref):
    t = pl.program_id(1)

    @pl.when((pl.program_id(0) == 0) & (t == 0))
    def _():
        carry_ref[...] = jnp.zeros_like(carry_ref)

    @pl.when(t == 0)
    def _():
        ext_ref[0:8, :] = jnp.zeros((8, C_WIDTH), F32)
        h_ref[...] = jnp.zeros_like(h_ref)

    u = u_ref[...].astype(F32)
    v = _rms(sv_ref[...].astype(F32), sgn_ref[...]).astype(BF16)
    tri = (lax.broadcasted_iota(jnp.int32, (B_BLOCK, B_BLOCK), 0)
           >= lax.broadcasted_iota(jnp.int32, (B_BLOCK, B_BLOCK), 1))
    blocks = []
    for blk in range(TM_MIX // B_BLOCK):
        rs = slice(blk * B_BLOCK, (blk + 1) * B_BLOCK)
        groups = []
        for g in range(B_GROUPS):
            cs = slice(g * LANES, (g + 1) * LANES)
            w = jnp.where(tri, sgw_ref[g], 0.0).astype(BF16)
            groups.append(jnp.dot(w, v[rs, cs], preferred_element_type=F32))
        blocks.append(jnp.concatenate(groups, axis=1) + sgb_ref[...])
    b_out = (u * jnp.concatenate(blocks, axis=0)).astype(BF16)

    xr = xr_ref[...].astype(F32)
    ext_ref[8:, :] = xr
    xc = cb_ref[...]
    for j in range(CONV_WIDTH):
        lo = 8 - (CONV_WIDTH - 1) + j
        xc = xc + cw_ref[j:j + 1, :] * ext_ref[lo:lo + TM_MIX, :]
    ext_ref[0:8, :] = xr[TM_MIX - 8:, :]
    ri = jnp.dot(xc.astype(BF16), wax_ref[...], preferred_element_type=F32) + bax_ref[...]
    r = _sigmoid(ri[:, :C_WIDTH])
    ig = _sigmoid(ri[:, C_WIDTH:])
    z = -lam_ref[...]
    softplus = jnp.maximum(z, 0.0) + jnp.log1p(jnp.exp(-jnp.abs(z)))
    log_a = -LRU_C * r * softplus
    a = jnp.exp(log_a)
    th = jnp.tanh(log_a)
    mult = jnp.sqrt(jnp.maximum(-2.0 * th / (1.0 - th), 0.0))
    h, h_ref[...] = _scan_linear(a, mult * (ig * xc), h_ref[...])
    c_out = (xg_ref[...].astype(F32) * h).astype(BF16)

    merged = None
    for k, br in enumerate((att_ref[...], b_out, c_out)):
        cs = slice(k * D_MODEL, (k + 1) * D_MODEL)
        term = gl_ref[:, cs].astype(F32) * jnp.dot(br, wbr_ref[k], preferred_element_type=F32)
        merged = term if merged is None else merged + term
    x_new = x_ref[...] + jnp.dot(merged.astype(BF16), wout_ref[...], preferred_element_type=F32)
    o_ref[...] = x_new

    meta, carry = _route(x_new, g2_ref[...], wr_ref, br_ref, carry_ref)
    carry_ref[...] = carry
    meta_ref[...] = meta
    cnt_ref[...] = jnp.broadcast_to(carry, cnt_ref.shape)


def _mix(proj, att, x, p, batch, seq):
    nt = seq // TM_MIX
    n = batch * seq

    def row(width, col):
        return pl.BlockSpec((TM_MIX, width), lambda b, t: (b * nt + t, col))

    def const(shape):
        return pl.BlockSpec(shape, lambda b, t: (0,) * len(shape))

    return pl.pallas_call(
        _mix_kernel,
        grid=(batch, nt),
        in_specs=[row(GATE_COLS, 0), row(COL_BLK, U_BLK), row(COL_BLK, SV_BLK), row(COL_BLK, XG_BLK),
                  row(COL_BLK, XR_BLK), row(A_WIDTH, 0), row(D_MODEL, 0),
                  const((1, B_WIDTH)), const((B_GROUPS, B_BLOCK, B_BLOCK)),
                  const((B_BLOCK, B_WIDTH)), const((CONV_WIDTH, C_WIDTH)), const((1, C_WIDTH)),
                  const((C_WIDTH, 2 * C_WIDTH)), const((1, 2 * C_WIDTH)), const((1, C_WIDTH)),
                  const((N_BRANCH, A_WIDTH, D_MODEL)), const((D_MODEL, D_MODEL)),
                  const((1, D_MODEL)), const((D_MODEL, LANES)), const((1, LANES))],
        out_specs=[row(D_MODEL, 0), row(LANES, 0), const((8, LANES))],
        out_shape=[jax.ShapeDtypeStruct((n, D_MODEL), F32),
                   jax.ShapeDtypeStruct((n, LANES), F32),
                   jax.ShapeDtypeStruct((8, LANES), F32)],
        scratch_shapes=[pltpu.VMEM((TM_MIX + 8, C_WIDTH), F32), pltpu.VMEM((1, C_WIDTH), F32),
                        pltpu.VMEM((1, LANES), F32)],
        compiler_params=_cparams(("arbitrary", "arbitrary")),
        name="mix",
    )(proj, proj, proj, proj, proj, att, x,
      p["sgu_norm_g"], p["sgu_w"], p["sgu_b"], p["conv_w"], p["conv_b"],
      p["w_ax"], p["b_ax"], p["lam"], p["w_branch"], p["w_out"], p["norm2_g"], p["w_r"], p["b_r"])


def _dispatch_kernel(d_ref, x_ref, g_ref, meta_ref, xs_in_ref, xs_ref, rows_ref, sem):
    del xs_in_ref
    rows_ref[:, :D_MODEL] = _rms(x_ref[...], g_ref[...])
    rows_ref[:, D_MODEL:] = meta_ref[...]

    def wait(r, c):
        _row_copy(rows_ref, 0, xs_ref, 0, sem).wait()
        return c

    for r in range(TT):
        _row_copy(rows_ref, r, xs_ref, d_ref[0, 0, r], sem).start()
    lax.fori_loop(0, TT, wait, 0, unroll=8)


def _dispatch(dest, x, g, meta, xs_init):
    n = x.shape[0]
    return pl.pallas_call(
        _dispatch_kernel,
        grid=(n // TT,),
        in_specs=[pl.BlockSpec((1, 1, TT), lambda i: (i, 0, 0), memory_space=pltpu.SMEM),
                  pl.BlockSpec((TT, D_MODEL), lambda i: (i, 0)),
                  pl.BlockSpec((1, D_MODEL), lambda i: (0, 0)),
                  pl.BlockSpec((TT, LANES), lambda i: (i, 0)),
                  pl.BlockSpec(memory_space=pl.ANY)],
        out_specs=pl.BlockSpec(memory_space=pl.ANY),
        out_shape=jax.ShapeDtypeStruct(xs_init.shape, F32),
        scratch_shapes=[pltpu.VMEM((TT, ROW_W), F32), pltpu.SemaphoreType.DMA(())],
        input_output_aliases={4: 0},
        compiler_params=_cparams(("arbitrary",)),
        name="dispatch",
    )(dest, x, g, meta, xs_init)


def _combine_kernel(d_ref, x_ref, ys_ref, g_ref, o_ref, buf_ref, sem):
    def wait(r, c):
        _row_copy(ys_ref, 0, buf_ref, 0, sem).wait()
        return c

    for r in range(TT):
        _row_copy(ys_ref, d_ref[0, 0, r], buf_ref, r, sem).start()
    lax.fori_loop(0, TT, wait, 0, unroll=8)
    o_ref[...] = _rms(x_ref[...] + buf_ref[...], g_ref[...])


def _combine(dest, x, ys, g):
    n = x.shape[0]
    return pl.pallas_call(
        _combine_kernel,
        grid=(n // TT,),
        in_specs=[pl.BlockSpec((1, 1, TT), lambda i: (i, 0, 0), memory_space=pltpu.SMEM),
                  pl.BlockSpec((TT, D_MODEL), lambda i: (i, 0)),
                  pl.BlockSpec(memory_space=pl.ANY),
                  pl.BlockSpec((1, D_MODEL), lambda i: (0, 0))],
        out_specs=pl.BlockSpec((TT, D_MODEL), lambda i: (i, 0)),
        out_shape=jax.ShapeDtypeStruct((n, D_MODEL), F32),
        scratch_shapes=[pltpu.VMEM((TT, D_MODEL), F32), pltpu.SemaphoreType.DMA(())],
        compiler_params=_cparams(("arbitrary",)),
        name="combine",
    )(dest, x, ys, g)


def _expert_kernel(ta_ref, tb_ref, tv_ref, xs_ref, wgu_a_ref, wd_a_ref, wgu_b_ref, wd_b_ref, o_ref):
    del ta_ref, tb_ref
    t = pl.program_id(0)

    @pl.when(tv_ref[t] == 0)
    def _():
        o_ref[...] = jnp.zeros_like(o_ref)

    @pl.when(tv_ref[t] != 0)
    def _():
        xb = xs_ref[:, :D_MODEL].astype(BF16)
        cw = xs_ref[:, D_MODEL:]

        def expert(wgu_ref, wd_ref):
            gu = jnp.dot(xb, wgu_ref[...], preferred_element_type=F32)
            hid = gu[:, :D_EXPERT] * _sigmoid(gu[:, :D_EXPERT]) * gu[:, D_EXPERT:]
            return jnp.dot(hid.astype(BF16), wd_ref[...], preferred_element_type=F32)

        o_ref[...] = cw[:, 2:3] * expert(wgu_a_ref, wd_a_ref) + cw[:, 3:4] * expert(wgu_b_ref, wd_b_ref)


def _experts(tile_a, tile_b, tile_v, xs, wgu, wd):
    rows = xs.shape[0]
    grid_spec = pltpu.PrefetchScalarGridSpec(
        num_scalar_prefetch=3,
        grid=(rows // TM_E,),
        in_specs=[pl.BlockSpec((TM_E, ROW_W), lambda t, a, b, v: (t, 0)),
                  pl.BlockSpec((None, D_MODEL, 2 * D_EXPERT), lambda t, a, b, v: (a[t], 0, 0)),
                  pl.BlockSpec((None, D_EXPERT, D_MODEL), lambda t, a, b, v: (a[t], 0, 0)),
                  pl.BlockSpec((None, D_MODEL, 2 * D_EXPERT), lambda t, a, b, v: (b[t], 0, 0)),
                  pl.BlockSpec((None, D_EXPERT, D_MODEL), lambda t, a, b, v: (b[t], 0, 0))],
        out_specs=pl.BlockSpec((TM_E, D_MODEL), lambda t, a, b, v: (t, 0)),
    )
    return pl.pallas_call(
        _expert_kernel,
        grid_spec=grid_spec,
        out_shape=jax.ShapeDtypeStruct((rows, D_MODEL), F32),
        compiler_params=_cparams(("arbitrary",)),
        name="experts",
    )(tile_a, tile_b, tile_v, xs, wgu, wd, wgu, wd)


def _plan(meta, counts, n_rows):
    bucket = meta[:, 0].astype(jnp.int32)
    rank = meta[:, 1].astype(jnp.int32)
    cnt = counts[0, :N_BUCKETS].astype(jnp.int32)
    padded = ((cnt + TM_E - 1) // TM_E) * TM_E
    ends = jnp.cumsum(padded)
    base = ends - padded
    ids = jnp.arange(N_BUCKETS, dtype=jnp.int32)
    dest = jnp.sum(jnp.where(bucket[:, None] == ids[None, :], base[None, :], 0), axis=1) + rank
    starts = jnp.arange(n_rows // TM_E, dtype=jnp.int32) * TM_E
    valid = starts < ends[-1]
    probe = jnp.minimum(starts, ends[-1] - 1)
    tile_bucket = jnp.minimum(jnp.sum((probe[:, None] >= ends[None, :]).astype(jnp.int32), axis=1), N_BUCKETS - 1)
    grp = tile_bucket // 6
    pair = tile_bucket % 6
    ge3 = (pair >= 3).astype(jnp.int32)
    ge5 = (pair >= 5).astype(jnp.int32)
    tile_a = grp * EXPERTS_PER_GROUP + ge3 + ge5
    tile_b = grp * EXPERTS_PER_GROUP + pair + 1 - 2 * ge3 - ge5
    return dest, tile_a, tile_b, valid.astype(jnp.int32)


def _block_diag(w):
    nb, d, _ = w.shape
    return jnp.einsum("hij,hg->higj", w, jnp.eye(nb, dtype=w.dtype)).reshape(nb * d, nb * d)


def kernel(x, norm1_g, w_in, b_gate, rel_bias, sgu_norm_g, sgu_w, sgu_b, conv_w, conv_b, rg_wa, rg_ba, rg_wx,
           rg_bx, rg_lambda, w_branch, w_out, norm2_g, router_grp_w, router_grp_b, router_exp_w, router_exp_b,
           exp_w_gate, exp_w_up, exp_w_down, final_norm_g):
    batch, seq, _ = x.shape
    depth = w_in.shape[0]
    n = batch * seq
    assert seq % TQ == 0 and seq % TM_MIX == 0 and n % TT == 0 and n % TM_IN == 0
    n_rows = n + N_BUCKETS * TM_E
    qkv_etc = 3 * A_WIDTH + 2 * B_WIDTH + 2 * C_WIDTH

    q_scale = (A_HEAD_DIM ** -0.5) * LOG2E
    w_in_r = jnp.concatenate([0.5 * w_in[:, :, qkv_etc:], q_scale * w_in[:, :, :A_WIDTH],
                              w_in[:, :, A_WIDTH:qkv_etc]], axis=2).astype(BF16)
    wgu = jnp.concatenate([exp_w_gate, exp_w_up], axis=-1).astype(BF16)
    wd = exp_w_down.astype(BF16)
    w_br = w_branch.astype(BF16)
    w_o = (0.5 * w_out).astype(BF16)
    n_logits = N_GROUPS + N_EXPERTS
    w_r32 = jnp.concatenate([router_grp_w, router_exp_w], axis=-1)
    w_r_hi = w_r32.astype(BF16)
    w_r_lo = (w_r32 - w_r_hi.astype(F32)).astype(BF16)
    zpad = lambda w, k: jnp.pad(w, ((0, 0), (0, 0), (0, k)))
    w_r = jnp.concatenate([zpad(w_r_hi, 32 - n_logits), zpad(w_r_lo, LANES - 32 - n_logits)], axis=-1)
    b_r = jnp.pad(jnp.concatenate([router_grp_b, router_exp_b], axis=-1), ((0, 0), (0, LANES - n_logits)))

    x2 = x.reshape(n, D_MODEL)
    xs_init = jnp.zeros((n_rows, ROW_W), F32)
    dest3 = ys = None
    for l in range(depth):
        p = {
            "sgu_norm_g": sgu_norm_g[l].reshape(1, B_WIDTH),
            "sgu_w": sgu_w[l],
            "sgu_b": jnp.repeat(sgu_b[l].T, LANES, axis=1),
            "conv_w": conv_w[l],
            "conv_b": conv_b[l].reshape(1, C_WIDTH),
            "w_ax": jnp.concatenate([_block_diag(rg_wa[l]), _block_diag(rg_wx[l])], axis=1).astype(BF16),
            "b_ax": jnp.concatenate([rg_ba[l], rg_bx[l]]).reshape(1, 2 * C_WIDTH),
            "lam": rg_lambda[l].reshape(1, C_WIDTH),
            "w_branch": w_br[l],
            "w_out": w_o[l],
            "norm2_g": norm2_g[l].reshape(1, D_MODEL),
            "w_r": w_r[l],
            "b_r": b_r[l].reshape(1, LANES),
        }
        x2, proj = _in_proj(x2, norm1_g[l].reshape(1, D_MODEL), 0.5 * b_gate[l].reshape(1, GATE_COLS), w_in_r[l],
                            dest3, ys)
        att = _attention(proj, _attn_bias_table(rel_bias[l]), batch, seq)
        x2, meta, counts = _mix(proj, att, x2, p, batch, seq)

        dest, tile_a, tile_b, tile_v = _plan(meta, counts, n_rows)
        dest3 = dest.reshape(n // TT, 1, TT)
        xs = _dispatch(dest3, x2, p["norm2_g"], meta, xs_init)
        ys = _experts(tile_a, tile_b, tile_v, xs, wgu[l], wd[l])
    x2 = _combine(dest3, x2, ys, final_norm_g.reshape(1, D_MODEL))
    return x2.reshape(batch, seq, D_MODEL)
```

```python
import jax
import jax.numpy as jnp
from jax import lax
from jax.experimental import pallas as pl
from jax.experimental.pallas import tpu as pltpu

F32 = jnp.float32
BF16 = jnp.bfloat16

D_MODEL = 1024
CHUNK = 64
A_HEADS = 8
A_HEAD_DIM = 64
A_WIDTH = 512
A_LEFT_CHUNKS = 8
REL_CLIP = 128
B_BLOCK = 128
B_GROUPS = 4
B_WIDTH = 512
C_WIDTH = 512
C_BLOCKS = 8
CONV_WIDTH = 4
LRU_C = 8.0
N_BRANCH = 3
GATE_COLS = N_BRANCH * D_MODEL
IN_COLS = 3 * A_WIDTH + 2 * B_WIDTH + 2 * C_WIDTH + GATE_COLS
N_GROUPS = 4
EXPERTS_PER_GROUP = 4
N_EXPERTS = 16
D_EXPERT = 512
EPS = 1e-6
NEG = -1e30
LOG2E = 1.4426950408889634

COL_BLK = 512
Q_BLK, K_BLK, V_BLK, U_BLK, SV_BLK, XG_BLK, XR_BLK = 6, 7, 8, 9, 10, 11, 12

TM_IN = 512
TQ = 256
TM_MIX = 256
TT = 512
TM_E = 256
N_BUCKETS = N_GROUPS * 6
LANES = 128
ROW_W = D_MODEL + LANES
VMEM_LIMIT = 56 * 1024 * 1024


def _cparams(sem):
    return pltpu.CompilerParams(dimension_semantics=sem, vmem_limit_bytes=VMEM_LIMIT)


def _rms(x, g):
    return x * lax.rsqrt(jnp.mean(x * x, axis=-1, keepdims=True) + EPS) * g


def _sigmoid(x):
    return 0.5 * jnp.tanh(0.5 * x) + 0.5


def _row_copy(src_ref, src_row, dst_ref, dst_row, sem):
    return pltpu.make_async_copy(src_ref.at[pl.ds(src_row, 1)], dst_ref.at[pl.ds(dst_row, 1)], sem)


def _gelu(x):
    c = 0.7978845608028654
    h = 0.5 * x
    return h + h * jnp.tanh(x * (c + (c * 0.044715) * (x * x)))


def _project(x, g_ref, bg_ref, w_ref, o_ref, between=None):
    xn = _rms(x, g_ref[...]).astype(BF16)
    for j in range(IN_COLS // COL_BLK):
        sl = slice(j * COL_BLK, (j + 1) * COL_BLK)
        acc = jnp.dot(xn, w_ref[:, sl], preferred_element_type=F32)
        if j < GATE_COLS // COL_BLK:
            acc = 1.0 + jnp.tanh(acc + bg_ref[:, sl])
        elif j in (U_BLK, SV_BLK, XG_BLK):
            acc = _gelu(acc)
        o_ref[:, sl] = acc.astype(BF16)
        if between is not None:
            between(j)


def _inproj_kernel(x_ref, g_ref, bg_ref, w_ref, o_ref):
    _project(x_ref[...], g_ref, bg_ref, w_ref, o_ref)


def _inproj_combine_kernel(dcur_ref, dnext_ref, x_ref, ys_ref, g_ref, bg_ref, w_ref, xo_ref, o_ref, buf_ref, sem):
    i = pl.program_id(0)
    slot = i % 2
    nxt = 1 - slot

    def gather(d_ref, s, r):
        return _row_copy(ys_ref, d_ref[0, 0, r], buf_ref.at[s], r, sem.at[s])

    def wait_rows(s):
        pltpu.make_async_copy(ys_ref.at[pl.ds(0, TM_IN)], buf_ref.at[s], sem.at[s]).wait()

    @pl.when(i == 0)
    def _():
        def body(r, c):
            gather(dcur_ref, 0, r).start()
            return c
        lax.fori_loop(0, TM_IN, body, 0, unroll=8)

    wait_rows(slot)
    x_new = x_ref[...] + buf_ref[slot]
    xo_ref[...] = x_new

    n_chunks = IN_COLS // COL_BLK
    per = -(-TM_IN // n_chunks)

    def issue(j):
        for r in range(j * per, min((j + 1) * per, TM_IN)):
            gather(dnext_ref, nxt, r).start()

    _project(x_new, g_ref, bg_ref, w_ref, o_ref, between=issue)

    @pl.when(i == pl.num_programs(0) - 1)
    def _():
        wait_rows(nxt)


def _in_proj(x, g, bg, w, dest3=None, ys=None):
    n = x.shape[0]
    nt = n // TM_IN
    x_spec = pl.BlockSpec((TM_IN, D_MODEL), lambda i: (i, 0))
    g_spec = pl.BlockSpec((1, D_MODEL), lambda i: (0, 0))
    bg_spec = pl.BlockSpec((1, GATE_COLS), lambda i: (0, 0))
    w_spec = pl.BlockSpec((D_MODEL, IN_COLS), lambda i: (0, 0), pipeline_mode=pl.Buffered(1))
    o_spec = pl.BlockSpec((TM_IN, IN_COLS), lambda i: (i, 0))
    o_shape = jax.ShapeDtypeStruct((n, IN_COLS), BF16)
    if ys is None:
        return x, pl.pallas_call(
            _inproj_kernel, grid=(nt,), in_specs=[x_spec, g_spec, bg_spec, w_spec], out_specs=o_spec,
            out_shape=o_shape, compiler_params=_cparams(("parallel",)), name="in_proj",
        )(x, g, bg, w)
    assert TM_IN == TT
    return pl.pallas_call(
        _inproj_combine_kernel,
        grid=(nt,),
        in_specs=[pl.BlockSpec((1, 1, TT), lambda i: (i, 0, 0), memory_space=pltpu.SMEM),
                  pl.BlockSpec((1, 1, TT), lambda i: (jnp.minimum(i + 1, nt - 1), 0, 0), memory_space=pltpu.SMEM),
                  x_spec, pl.BlockSpec(memory_space=pl.ANY), g_spec, bg_spec, w_spec],
        out_specs=[x_spec, o_spec],
        out_shape=[jax.ShapeDtypeStruct((n, D_MODEL), F32), o_shape],
        scratch_shapes=[pltpu.VMEM((2, TM_IN, D_MODEL), F32), pltpu.SemaphoreType.DMA((2,))],
        compiler_params=_cparams(("arbitrary",)),
        name="in_proj_combine",
    )(dest3, dest3, x, ys, g, bg, w)


def _attn_kernel(q_ref, k0_ref, k1_ref, k2_ref, v0_ref, v1_ref, v2_ref, bias_ref, o_ref):
    lane = lax.broadcasted_iota(jnp.int32, (TQ, LANES), 1)
    first = lane < A_HEAD_DIM
    for p in range(A_HEADS // 2):
        sl = slice(p * LANES, (p + 1) * LANES)
        qp = q_ref[:, sl]
        kp = jnp.concatenate([k0_ref[:, sl], k1_ref[:, sl], k2_ref[:, sl]], axis=0)
        vp = jnp.concatenate([v0_ref[:, sl], v1_ref[:, sl], v2_ref[:, sl]], axis=0)
        zero = jnp.zeros_like(qp)
        q2 = jnp.concatenate([jnp.where(first, qp, zero), jnp.where(first, zero, qp)], axis=0)
        s = lax.dot_general(q2, kp, (((1,), (1,)), ((), ())), preferred_element_type=F32)
        s = s + bias_ref[p]
        e = jnp.exp2(s - jnp.max(s, axis=-1, keepdims=True))
        l = jnp.sum(e, axis=-1, keepdims=True)
        o2 = jnp.dot(e.astype(BF16), vp, preferred_element_type=F32) / l
        o_ref[:, sl] = jnp.where(first, o2[:TQ], o2[TQ:]).astype(BF16)


def _attention(proj, bias, batch, seq):
    nq = seq // TQ

    def blk(col, back):
        return pl.BlockSpec((TQ, COL_BLK), lambda b, i: (b * nq + jnp.maximum(i - back, 0), col))

    return pl.pallas_call(
        _attn_kernel,
        grid=(batch, nq),
        in_specs=[blk(Q_BLK, 0), blk(K_BLK, 2), blk(K_BLK, 1), blk(K_BLK, 0),
                  blk(V_BLK, 2), blk(V_BLK, 1), blk(V_BLK, 0),
                  pl.BlockSpec((None, A_HEADS // 2, 2 * TQ, 3 * TQ), lambda b, i: (jnp.minimum(i, 2), 0, 0, 0))],
        out_specs=pl.BlockSpec((TQ, A_WIDTH), lambda b, i: (b * nq + i, 0)),
        out_shape=jax.ShapeDtypeStruct((batch * seq, A_WIDTH), BF16),
        compiler_params=_cparams(("parallel", "parallel")),
        name="attention",
    )(proj, proj, proj, proj, proj, proj, proj, bias)


def _attn_bias_table(rel_bias):
    heads = rel_bias.shape[0]
    rb = rel_bias.astype(F32) * LOG2E
    width = 4 * TQ
    n_head = 3 * TQ - 1 - REL_CLIP
    n_tail = width - n_head - (2 * REL_CLIP + 1)
    diag = jnp.concatenate([jnp.broadcast_to(rb[:, -1:], (heads, n_head)), rb[:, ::-1],
                            jnp.broadcast_to(rb[:, :1], (heads, n_tail))], axis=1)
    rolled = jnp.roll(diag, -(TQ - 1), axis=1)
    flat = jnp.broadcast_to(rolled[:, None, :], (heads, TQ, width)).reshape(heads, TQ * width)
    toe = flat[:, :TQ * (width - 1)].reshape(heads, TQ, width - 1)[:, :, :3 * TQ]
    q = jnp.arange(TQ)
    k = jnp.arange(3 * TQ)
    qc = (q // CHUNK)[:, None]
    kc = (k // CHUNK)[None, :]
    band = (kc >= qc) & (kc <= qc + A_LEFT_CHUNKS)
    variants = []
    for v in range(3):
        live = band & (k[None, :] >= (2 - v) * TQ)
        variants.append(jnp.where(live[None], toe, NEG))
    return jnp.stack(variants).reshape(3, heads // 2, 2 * TQ, 3 * TQ)


def _route(x_new, g2, wr_ref, br_ref, carry_ref):
    rows = x_new.shape[0]
    xn = _rms(x_new, g2)
    x_hi = xn.astype(BF16)
    x_lo = (xn - x_hi.astype(F32)).astype(BF16)
    p_hi = jnp.dot(x_hi, wr_ref[...], preferred_element_type=F32)
    p_lo = jnp.dot(x_lo, wr_ref[...], preferred_element_type=F32)
    lg = p_hi + pltpu.roll(p_hi, LANES - 32, 1) + p_lo + br_ref[...]
    lane = lax.broadcasted_iota(jnp.int32, (rows, LANES), 1).astype(F32)

    def top(mask):
        best = jnp.max(jnp.where(mask, lg, -jnp.inf), axis=-1, keepdims=True)
        idx = jnp.min(jnp.where(mask & (lg == best), lane, float(LANES)), axis=-1, keepdims=True)
        return best, idx

    gmask = lane < N_GROUPS
    gmax, grp = top(gmask)
    g_w = 1.0 / jnp.sum(jnp.where(gmask, jnp.exp(lg - gmax), 0.0), axis=-1, keepdims=True)
    lo = N_GROUPS + EXPERTS_PER_GROUP * grp
    in_grp = (lane >= lo) & (lane < lo + EXPERTS_PER_GROUP)
    l1, i1 = top(in_grp)
    l2, i2 = top(in_grp & (lane != i1))
    e2 = jnp.exp(l2 - l1)
    w1 = g_w / (1.0 + e2)
    w2 = g_w * e2 / (1.0 + e2)
    first_is_low = i1 < i2
    cw_a = jnp.where(first_is_low, w1, w2)
    cw_b = jnp.where(first_is_low, w2, w1)
    la = jnp.minimum(i1, i2) - lo
    lb = jnp.maximum(i1, i2) - lo
    bucket = grp * 6.0 + la * (7.0 - la) * 0.5 + (lb - la - 1.0)

    onehot = lane == bucket
    strict = (lax.broadcasted_iota(jnp.int32, (rows, rows), 0) > lax.broadcasted_iota(jnp.int32, (rows, rows), 1))
    prefix = jnp.dot(jnp.where(strict, 1.0, 0.0).astype(BF16), jnp.where(onehot, 1.0, 0.0).astype(BF16),
                     preferred_element_type=F32)
    carry = carry_ref[...]
    rank = jnp.sum(jnp.where(onehot, prefix + carry, 0.0), axis=-1, keepdims=True)
    carry = carry + jnp.sum(jnp.where(onehot, 1.0, 0.0), axis=0, keepdims=True)
    meta = jnp.where(lane == 0.0, bucket, jnp.where(lane == 1.0, rank, jnp.where(lane == 2.0, cw_a,
                     jnp.where(lane == 3.0, cw_b, 0.0))))
    return xn, meta, carry


def _scan_linear(a, b, h_prev):
    n, c = a.shape
    groups = n // 8
    a = a.reshape(groups, 8, c)
    b = b.reshape(groups, 8, c)
    sub = lax.broadcasted_iota(jnp.int32, a.shape, 1)
    for s in (1, 2, 4):
        keep = sub >= s
        b = a * jnp.where(keep, pltpu.roll(b, s, 1), 0.0) + b
        a = a * jnp.where(keep, pltpu.roll(a, s, 1), 1.0)
    outs = []
    for g in range(groups):
        hg = a[g] * h_prev + b[g]
        outs.append(hg)
        h_prev = hg[7:8, :]
    return jnp.concatenate(outs, axis=0), h_prev


def _mix_kernel(gl_ref, u_ref, sv_ref, xg_ref, xr_ref, att_ref, x_ref,
                sgn_ref, sgw_ref, sgb_ref, cw_ref, cb_ref, wax_ref, bax_ref, lam_ref,
                wbr_ref, wout_ref, g2_ref, wr_ref, br_ref,
                o_ref, rows_ref, meta_ref, cnt_ref, ext_ref, h_ref, carry_ref):
    t = pl.program_id(1)

    @pl.when((pl.program_id(0) == 0) & (t == 0))
    def _():
        carry_ref[...] = jnp.zeros_like(carry_ref)

    @pl.when(t == 0)
    def _():
        ext_ref[0:8, :] = jnp.zeros((8, C_WIDTH), F32)
        h_ref[...] = jnp.zeros_like(h_ref)

    u = u_ref[...].astype(F32)
    v = _rms(sv_ref[...].astype(F32), sgn_ref[...]).astype(BF16)
    tri = (lax.broadcasted_iota(jnp.int32, (B_BLOCK, B_BLOCK), 0)
           >= lax.broadcasted_iota(jnp.int32, (B_BLOCK, B_BLOCK), 1))
    blocks = []
    for blk in range(TM_MIX // B_BLOCK):
        rs = slice(blk * B_BLOCK, (blk + 1) * B_BLOCK)
        groups = []
        for g in range(B_GROUPS):
            cs = slice(g * LANES, (g + 1) * LANES)
            w = jnp.where(tri, sgw_ref[g], 0.0).astype(BF16)
            groups.append(jnp.dot(w, v[rs, cs], preferred_element_type=F32))
        blocks.append(jnp.concatenate(groups, axis=1) + sgb_ref[...])
    b_out = (u * jnp.concatenate(blocks, axis=0)).astype(BF16)

    xr = xr_ref[...].astype(F32)
    ext_ref[8:, :] = xr
    xc = cb_ref[...]
    for j in range(CONV_WIDTH):
        lo = 8 - (CONV_WIDTH - 1) + j
        xc = xc + cw_ref[j:j + 1, :] * ext_ref[lo:lo + TM_MIX, :]
    ext_ref[0:8, :] = xr[TM_MIX - 8:, :]
    ri = jnp.dot(xc.astype(BF16), wax_ref[...], preferred_element_type=F32) + bax_ref[...]
    r = _sigmoid(ri[:, :C_WIDTH])
    ig = _sigmoid(ri[:, C_WIDTH:])
    z = -lam_ref[...]
    softplus = jnp.maximum(z, 0.0) + jnp.log1p(jnp.exp(-jnp.abs(z)))
    log_a = -LRU_C * r * softplus
    a = jnp.exp(log_a)
    th = jnp.tanh(log_a)
    mult = jnp.sqrt(jnp.maximum(-2.0 * th / (1.0 - th), 0.0))
    h, h_ref[...] = _scan_linear(a, mult * (ig * xc), h_ref[...])
    c_out = (xg_ref[...].astype(F32) * h).astype(BF16)

    merged = None
    for k, br in enumerate((att_ref[...], b_out, c_out)):
        cs = slice(k * D_MODEL, (k + 1) * D_MODEL)
        term = gl_ref[:, cs].astype(F32) * jnp.dot(br, wbr_ref[k], preferred_element_type=F32)
        merged = term if merged is None else merged + term
    x_new = x_ref[...] + jnp.dot(merged.astype(BF16), wout_ref[...], preferred_element_type=F32)
    o_ref[...] = x_new

    xn, meta, carry = _route(x_new, g2_ref[...], wr_ref, br_ref, carry_ref)
    carry_ref[...] = carry
    meta_ref[...] = meta
    rows_ref[:, :D_MODEL] = xn
    rows_ref[:, D_MODEL:] = meta
    cnt_ref[...] = jnp.broadcast_to(carry, cnt_ref.shape)


def _mix(proj, att, x, p, batch, seq):
    nt = seq // TM_MIX
    n = batch * seq

    def row(width, col):
        return pl.BlockSpec((TM_MIX, width), lambda b, t: (b * nt + t, col))

    def const(shape):
        return pl.BlockSpec(shape, lambda b, t: (0,) * len(shape))

    return pl.pallas_call(
        _mix_kernel,
        grid=(batch, nt),
        in_specs=[row(GATE_COLS, 0), row(COL_BLK, U_BLK), row(COL_BLK, SV_BLK), row(COL_BLK, XG_BLK),
                  row(COL_BLK, XR_BLK), row(A_WIDTH, 0), row(D_MODEL, 0),
                  const((1, B_WIDTH)), const((B_GROUPS, B_BLOCK, B_BLOCK)),
                  const((B_BLOCK, B_WIDTH)), const((CONV_WIDTH, C_WIDTH)), const((1, C_WIDTH)),
                  const((C_WIDTH, 2 * C_WIDTH)), const((1, 2 * C_WIDTH)), const((1, C_WIDTH)),
                  const((N_BRANCH, A_WIDTH, D_MODEL)), const((D_MODEL, D_MODEL)),
                  const((1, D_MODEL)), const((D_MODEL, LANES)), const((1, LANES))],
        out_specs=[row(D_MODEL, 0), row(ROW_W, 0), row(LANES, 0), const((8, LANES))],
        out_shape=[jax.ShapeDtypeStruct((n, D_MODEL), F32),
                   jax.ShapeDtypeStruct((n, ROW_W), F32),
                   jax.ShapeDtypeStruct((n, LANES), F32),
                   jax.ShapeDtypeStruct((8, LANES), F32)],
        scratch_shapes=[pltpu.VMEM((TM_MIX + 8, C_WIDTH), F32), pltpu.VMEM((1, C_WIDTH), F32),
                        pltpu.VMEM((1, LANES), F32)],
        compiler_params=_cparams(("arbitrary", "arbitrary")),
        name="mix",
    )(proj, proj, proj, proj, proj, att, x,
      p["sgu_norm_g"], p["sgu_w"], p["sgu_b"], p["conv_w"], p["conv_b"],
      p["w_ax"], p["b_ax"], p["lam"], p["w_branch"], p["w_out"], p["norm2_g"], p["w_r"], p["b_r"])


def _invmap_kernel(d_ref, inv_ref):
    i = pl.program_id(0)

    @pl.when(i == 0)
    def _():
        def clear(r, c):
            inv_ref[r] = 0
            return c
        lax.fori_loop(0, inv_ref.shape[0], clear, 0, unroll=8)

    def put(r, c):
        inv_ref[d_ref[0, 0, r]] = i * TT + r
        return c
    lax.fori_loop(0, TT, put, 0, unroll=8)


def _invmap(dest3, n_rows):
    return pl.pallas_call(
        _invmap_kernel,
        grid=(dest3.shape[0],),
        in_specs=[pl.BlockSpec((1, 1, TT), lambda i: (i, 0, 0), memory_space=pltpu.SMEM)],
        out_specs=pl.BlockSpec(memory_space=pltpu.SMEM),
        out_shape=jax.ShapeDtypeStruct((n_rows,), jnp.int32),
        compiler_params=_cparams(("arbitrary",)),
        name="invmap",
    )(dest3)


def _combine_kernel(d_ref, x_ref, ys_ref, g_ref, o_ref, buf_ref, sem):
    for r in range(TT):
        _row_copy(ys_ref, d_ref[0, 0, r], buf_ref, r, sem).start()
    pltpu.make_async_copy(ys_ref.at[pl.ds(0, TT)], buf_ref, sem).wait()
    o_ref[...] = _rms(x_ref[...] + buf_ref[...], g_ref[...])


def _combine(dest, x, ys, g):
    n = x.shape[0]
    return pl.pallas_call(
        _combine_kernel,
        grid=(n // TT,),
        in_specs=[pl.BlockSpec((1, 1, TT), lambda i: (i, 0, 0), memory_space=pltpu.SMEM),
                  pl.BlockSpec((TT, D_MODEL), lambda i: (i, 0)),
                  pl.BlockSpec(memory_space=pl.ANY),
                  pl.BlockSpec((1, D_MODEL), lambda i: (0, 0))],
        out_specs=pl.BlockSpec((TT, D_MODEL), lambda i: (i, 0)),
        out_shape=jax.ShapeDtypeStruct((n, D_MODEL), F32),
        scratch_shapes=[pltpu.VMEM((TT, D_MODEL), F32), pltpu.SemaphoreType.DMA(())],
        compiler_params=_cparams(("arbitrary",)),
        name="combine",
    )(dest, x, ys, g)


def _expert_kernel(ta_ref, tb_ref, tv_ref, icur_ref, inext_ref, rows_ref,
                   wgu_a_ref, wd_a_ref, wgu_b_ref, wd_b_ref, o_ref, buf_ref, sem):
    del ta_ref, tb_ref
    t = pl.program_id(0)
    last = pl.num_programs(0) - 1
    slot = t % 2
    nxt = 1 - slot

    def gather(i_ref, s, r):
        return _row_copy(rows_ref, i_ref[0, 0, r], buf_ref.at[s], r, sem.at[s])

    def wait_slot(s):
        pltpu.make_async_copy(rows_ref.at[pl.ds(0, TM_E)], buf_ref.at[s], sem.at[s]).wait()

    @pl.when(t == 0)
    def _():
        def body(r, c):
            gather(icur_ref, 0, r).start()
            return c
        lax.fori_loop(0, TM_E, body, 0, unroll=8)

    valid = tv_ref[t] != 0

    @pl.when(valid)
    def _():
        wait_slot(slot)
        xb = buf_ref[slot, :, :D_MODEL].astype(BF16)
        cw = buf_ref[slot, :, D_MODEL:]
        quarter = TM_E // 4

        def issue(part):
            for r in range(part * quarter, (part + 1) * quarter):
                gather(inext_ref, nxt, r).start()

        def expert(wgu_ref, wd_ref, part):
            gu = jnp.dot(xb, wgu_ref[...], preferred_element_type=F32)
            issue(part)
            hid = gu[:, :D_EXPERT] * _sigmoid(gu[:, :D_EXPERT]) * gu[:, D_EXPERT:]
            y = jnp.dot(hid.astype(BF16), wd_ref[...], preferred_element_type=F32)
            issue(part + 1)
            return y

        o_ref[...] = cw[:, 2:3] * expert(wgu_a_ref, wd_a_ref, 0) + cw[:, 3:4] * expert(wgu_b_ref, wd_b_ref, 2)

        @pl.when(t == last)
        def _():
            wait_slot(nxt)

    @pl.when(jnp.logical_not(valid))
    def _():
        o_ref[...] = jnp.zeros_like(o_ref)

        @pl.when(tv_ref[jnp.maximum(t - 1, 0)] != 0)
        def _():
            wait_slot(slot)


def _experts(tile_a, tile_b, tile_v, inv3, rows, wgu, wd):
    nt = inv3.shape[0]
    grid_spec = pltpu.PrefetchScalarGridSpec(
        num_scalar_prefetch=3,
        grid=(nt,),
        in_specs=[pl.BlockSpec((1, 1, TM_E), lambda t, a, b, v: (t, 0, 0), memory_space=pltpu.SMEM),
                  pl.BlockSpec((1, 1, TM_E), lambda t, a, b, v: (jnp.minimum(t + 1, nt - 1), 0, 0),
                               memory_space=pltpu.SMEM),
                  pl.BlockSpec(memory_space=pl.ANY),
                  pl.BlockSpec((None, D_MODEL, 2 * D_EXPERT), lambda t, a, b, v: (a[t], 0, 0)),
                  pl.BlockSpec((None, D_EXPERT, D_MODEL), lambda t, a, b, v: (a[t], 0, 0)),
                  pl.BlockSpec((None, D_MODEL, 2 * D_EXPERT), lambda t, a, b, v: (b[t], 0, 0)),
                  pl.BlockSpec((None, D_EXPERT, D_MODEL), lambda t, a, b, v: (b[t], 0, 0))],
        out_specs=pl.BlockSpec((TM_E, D_MODEL), lambda t, a, b, v: (t, 0)),
        scratch_shapes=[pltpu.VMEM((2, TM_E, ROW_W), F32), pltpu.SemaphoreType.DMA((2,))],
    )
    return pl.pallas_call(
        _expert_kernel,
        grid_spec=grid_spec,
        out_shape=jax.ShapeDtypeStruct((nt * TM_E, D_MODEL), F32),
        compiler_params=_cparams(("arbitrary",)),
        name="experts",
    )(tile_a, tile_b, tile_v, inv3, inv3, rows, wgu, wd, wgu, wd)


def _plan(meta, counts, n_rows):
    bucket = meta[:, 0].astype(jnp.int32)
    rank = meta[:, 1].astype(jnp.int32)
    cnt = counts[0, :N_BUCKETS].astype(jnp.int32)
    padded = ((cnt + TM_E - 1) // TM_E) * TM_E
    ends = jnp.cumsum(padded)
    base = ends - padded
    ids = jnp.arange(N_BUCKETS, dtype=jnp.int32)
    dest = jnp.sum(jnp.where(bucket[:, None] == ids[None, :], base[None, :], 0), axis=1) + rank
    starts = jnp.arange(n_rows // TM_E, dtype=jnp.int32) * TM_E
    valid = starts < ends[-1]
    probe = jnp.minimum(starts, ends[-1] - 1)
    tile_bucket = jnp.minimum(jnp.sum((probe[:, None] >= ends[None, :]).astype(jnp.int32), axis=1), N_BUCKETS - 1)
    grp = tile_bucket // 6
    pair = tile_bucket % 6
    ge3 = (pair >= 3).astype(jnp.int32)
    ge5 = (pair >= 5).astype(jnp.int32)
    tile_a = grp * EXPERTS_PER_GROUP + ge3 + ge5
    tile_b = grp * EXPERTS_PER_GROUP + pair + 1 - 2 * ge3 - ge5
    return dest, tile_a, tile_b, valid.astype(jnp.int32)


def _block_diag(w):
    nb, d, _ = w.shape
    return jnp.einsum("hij,hg->higj", w, jnp.eye(nb, dtype=w.dtype)).reshape(nb * d, nb * d)


def kernel(x, norm1_g, w_in, b_gate, rel_bias, sgu_norm_g, sgu_w, sgu_b, conv_w, conv_b, rg_wa, rg_ba, rg_wx,
           rg_bx, rg_lambda, w_branch, w_out, norm2_g, router_grp_w, router_grp_b, router_exp_w, router_exp_b,
           exp_w_gate, exp_w_up, exp_w_down, final_norm_g):
    batch, seq, _ = x.shape
    depth = w_in.shape[0]
    n = batch * seq
    assert seq % TQ == 0 and seq % TM_MIX == 0 and n % TT == 0 and n % TM_IN == 0
    n_rows = n + N_BUCKETS * TM_E
    qkv_etc = 3 * A_WIDTH + 2 * B_WIDTH + 2 * C_WIDTH

    q_scale = (A_HEAD_DIM ** -0.5) * LOG2E
    w_in_r = jnp.concatenate([0.5 * w_in[:, :, qkv_etc:], q_scale * w_in[:, :, :A_WIDTH],
                              w_in[:, :, A_WIDTH:qkv_etc]], axis=2).astype(BF16)
    wgu = jnp.concatenate([exp_w_gate, exp_w_up], axis=-1).astype(BF16)
    wd = exp_w_down.astype(BF16)
    w_br = w_branch.astype(BF16)
    w_o = (0.5 * w_out).astype(BF16)
    n_logits = N_GROUPS + N_EXPERTS
    w_r32 = jnp.concatenate([router_grp_w, router_exp_w], axis=-1)
    w_r_hi = w_r32.astype(BF16)
    w_r_lo = (w_r32 - w_r_hi.astype(F32)).astype(BF16)
    zpad = lambda w, k: jnp.pad(w, ((0, 0), (0, 0), (0, k)))
    w_r = jnp.concatenate([zpad(w_r_hi, 32 - n_logits), zpad(w_r_lo, LANES - 32 - n_logits)], axis=-1)
    b_r = jnp.pad(jnp.concatenate([router_grp_b, router_exp_b], axis=-1), ((0, 0), (0, LANES - n_logits)))

    x2 = x.reshape(n, D_MODEL)
    dest3 = ys = None
    for l in range(depth):
        p = {
            "sgu_norm_g": sgu_norm_g[l].reshape(1, B_WIDTH),
            "sgu_w": sgu_w[l],
            "sgu_b": jnp.repeat(sgu_b[l].T, LANES, axis=1),
            "conv_w": conv_w[l],
            "conv_b": conv_b[l].reshape(1, C_WIDTH),
            "w_ax": jnp.concatenate([_block_diag(rg_wa[l]), _block_diag(rg_wx[l])], axis=1).astype(BF16),
            "b_ax": jnp.concatenate([rg_ba[l], rg_bx[l]]).reshape(1, 2 * C_WIDTH),
            "lam": rg_lambda[l].reshape(1, C_WIDTH),
            "w_branch": w_br[l],
            "w_out": w_o[l],
            "norm2_g": norm2_g[l].reshape(1, D_MODEL),
            "w_r": w_r[l],
            "b_r": b_r[l].reshape(1, LANES),
        }
        x2, proj = _in_proj(x2, norm1_g[l].reshape(1, D_MODEL), 0.5 * b_gate[l].reshape(1, GATE_COLS), w_in_r[l],
                            dest3, ys)
        att = _attention(proj, _attn_bias_table(rel_bias[l]), batch, seq)
        x2, rows, meta, counts = _mix(proj, att, x2, p, batch, seq)

        dest, tile_a, tile_b, tile_v = _plan(meta, counts, n_rows)
        dest3 = dest.reshape(n // TT, 1, TT)
        inv3 = _invmap(dest3, n_rows).reshape(n_rows // TM_E, 1, TM_E)
        ys = _experts(tile_a, tile_b, tile_v, inv3, rows, wgu[l], wd[l])
    x2 = _combine(dest3, x2, ys, final_norm_g.reshape(1, D_MODEL))
    return x2.reshape(batch, seq, D_MODEL)
```

```python
import functools

import jax
import jax.numpy as jnp
from jax import lax
from jax.experimental import pallas as pl
from jax.experimental.pallas import tpu as pltpu

F32 = jnp.float32
BF16 = jnp.bfloat16

D_MODEL = 1024
CHUNK = 64
A_HEADS = 8
A_HEAD_DIM = 64
A_WIDTH = 512
A_LEFT_CHUNKS = 8
REL_CLIP = 128
B_BLOCK = 128
B_GROUPS = 4
B_WIDTH = 512
C_WIDTH = 512
C_BLOCKS = 8
CONV_WIDTH = 4
LRU_C = 8.0
N_BRANCH = 3
GATE_COLS = N_BRANCH * D_MODEL
IN_COLS = 3 * A_WIDTH + 2 * B_WIDTH + 2 * C_WIDTH + GATE_COLS
N_GROUPS = 4
EXPERTS_PER_GROUP = 4
N_EXPERTS = 16
D_EXPERT = 512
EPS = 1e-6
NEG = -1e30
LOG2E = 1.4426950408889634

COL_BLK = 512
Q_BLK, K_BLK, V_BLK, U_BLK, SV_BLK, XG_BLK, XR_BLK = 6, 7, 8, 9, 10, 11, 12

TM_IN = 512
TQ = 256
TM_MIX = 256
TT = 512
TM_E = 256
N_BUCKETS = N_GROUPS * 6
LANES = 128
ROW_W = D_MODEL + LANES
VMEM_LIMIT = 56 * 1024 * 1024


def _cparams(sem):
    return pltpu.CompilerParams(dimension_semantics=sem, vmem_limit_bytes=VMEM_LIMIT)


def _rms(x, g):
    return x * lax.rsqrt(jnp.mean(x * x, axis=-1, keepdims=True) + EPS) * g


def _sigmoid(x):
    return 0.5 * jnp.tanh(0.5 * x) + 0.5


def _row_copy(src_ref, src_row, dst_ref, dst_row, sem):
    return pltpu.make_async_copy(src_ref.at[pl.ds(src_row, 1)], dst_ref.at[pl.ds(dst_row, 1)], sem)


def _gelu(x):
    c = 0.7978845608028654
    h = 0.5 * x
    return h + h * jnp.tanh(x * (c + (c * 0.044715) * (x * x)))


def _project(x, g_ref, bg_ref, sgn_ref, cw_ref, cb_ref, w_ref, o_ref, halo_ref, between=None):
    xn = _rms(x, g_ref[...]).astype(BF16)
    heavy = [XR_BLK, SV_BLK, U_BLK, XG_BLK]
    order = heavy + [j for j in range(IN_COLS // COL_BLK) if j not in heavy]
    for step, j in enumerate(order):
        sl = slice(j * COL_BLK, (j + 1) * COL_BLK)
        acc = jnp.dot(xn, w_ref[:, sl], preferred_element_type=F32)
        if j < GATE_COLS // COL_BLK:
            acc = 1.0 + jnp.tanh(acc + bg_ref[:, sl])
        elif j in (U_BLK, XG_BLK):
            acc = _gelu(acc)
        elif j == SV_BLK:
            acc = _rms(_gelu(acc), sgn_ref[...])
        elif j == XR_BLK:
            ext = jnp.concatenate([halo_ref[...], acc], axis=0)
            halo_ref[...] = acc[TM_IN - 8:, :]
            conv = cb_ref[...]
            for k in range(CONV_WIDTH):
                lo = 8 - (CONV_WIDTH - 1) + k
                conv = conv + cw_ref[k:k + 1, :] * ext[lo:lo + TM_IN, :]
            acc = conv
        o_ref[:, sl] = acc.astype(BF16)
        if between is not None:
            between(step)


def _reset_halo(halo_ref, tiles_per_seq):
    @pl.when(pl.program_id(0) % tiles_per_seq == 0)
    def _():
        halo_ref[...] = jnp.zeros_like(halo_ref)


def _inproj_kernel(tiles_per_seq, x_ref, g_ref, bg_ref, sgn_ref, cw_ref, cb_ref, w_ref, o_ref, halo_ref):
    _reset_halo(halo_ref, tiles_per_seq)
    _project(x_ref[...], g_ref, bg_ref, sgn_ref, cw_ref, cb_ref, w_ref, o_ref, halo_ref)


def _inproj_combine_kernel(tiles_per_seq, dcur_ref, dnext_ref, x_ref, ys_ref, g_ref, bg_ref, sgn_ref, cw_ref, cb_ref,
                           w_ref, xo_ref, o_ref, buf_ref, halo_ref, sem):
    i = pl.program_id(0)
    slot = i % 2
    nxt = 1 - slot

    def gather(d_ref, s, r):
        return _row_copy(ys_ref, d_ref[0, 0, r], buf_ref.at[s], r, sem.at[s])

    def wait_rows(s):
        pltpu.make_async_copy(ys_ref.at[pl.ds(0, TM_IN)], buf_ref.at[s], sem.at[s]).wait()

    @pl.when(i == 0)
    def _():
        def body(r, c):
            gather(dcur_ref, 0, r).start()
            return c
        lax.fori_loop(0, TM_IN, body, 0, unroll=8)

    wait_rows(slot)
    x_new = x_ref[...] + buf_ref[slot]
    xo_ref[...] = x_new

    n_chunks = IN_COLS // COL_BLK
    per = -(-TM_IN // n_chunks)

    def issue(j):
        for r in range(j * per, min((j + 1) * per, TM_IN)):
            gather(dnext_ref, nxt, r).start()

    _reset_halo(halo_ref, tiles_per_seq)
    _project(x_new, g_ref, bg_ref, sgn_ref, cw_ref, cb_ref, w_ref, o_ref, halo_ref, between=issue)

    @pl.when(i == pl.num_programs(0) - 1)
    def _():
        wait_rows(nxt)


def _in_proj(x, p, seq, dest3=None, ys=None):
    n = x.shape[0]
    nt = n // TM_IN
    tiles_per_seq = seq // TM_IN

    def const(shape, **kw):
        return pl.BlockSpec(shape, lambda i: (0,) * len(shape), **kw)

    x_spec = pl.BlockSpec((TM_IN, D_MODEL), lambda i: (i, 0))
    params = [p["norm1_g"], p["b_gate_half"], p["sgu_norm_g"], p["conv_w"], p["conv_b"], p["w_in"]]
    param_specs = [const((1, D_MODEL)), const((1, GATE_COLS)), const((1, B_WIDTH)), const((CONV_WIDTH, C_WIDTH)),
                   const((1, C_WIDTH)), const((D_MODEL, IN_COLS), pipeline_mode=pl.Buffered(1))]
    o_spec = pl.BlockSpec((TM_IN, IN_COLS), lambda i: (i, 0))
    o_shape = jax.ShapeDtypeStruct((n, IN_COLS), BF16)
    halo = pltpu.VMEM((8, C_WIDTH), F32)
    if ys is None:
        return x, pl.pallas_call(
            functools.partial(_inproj_kernel, tiles_per_seq), grid=(nt,), in_specs=[x_spec] + param_specs,
            out_specs=o_spec, out_shape=o_shape, scratch_shapes=[halo],
            compiler_params=_cparams(("arbitrary",)), name="in_proj",
        )(x, *params)
    assert TM_IN == TT
    return pl.pallas_call(
        functools.partial(_inproj_combine_kernel, tiles_per_seq),
        grid=(nt,),
        in_specs=[pl.BlockSpec((1, 1, TT), lambda i: (i, 0, 0), memory_space=pltpu.SMEM),
                  pl.BlockSpec((1, 1, TT), lambda i: (jnp.minimum(i + 1, nt - 1), 0, 0), memory_space=pltpu.SMEM),
                  x_spec, pl.BlockSpec(memory_space=pl.ANY)] + param_specs,
        out_specs=[x_spec, o_spec],
        out_shape=[jax.ShapeDtypeStruct((n, D_MODEL), F32), o_shape],
        scratch_shapes=[pltpu.VMEM((2, TM_IN, D_MODEL), F32), halo, pltpu.SemaphoreType.DMA((2,))],
        compiler_params=_cparams(("arbitrary",)),
        name="in_proj_combine",
    )(dest3, dest3, x, ys, *params)


def _attn_kernel(q_ref, k0_ref, k1_ref, k2_ref, v0_ref, v1_ref, v2_ref, bias_ref, o_ref):
    lane = lax.broadcasted_iota(jnp.int32, (TQ, LANES), 1)
    first = lane < A_HEAD_DIM
    for p in range(A_HEADS // 2):
        sl = slice(p * LANES, (p + 1) * LANES)
        qp = q_ref[:, sl]
        kp = jnp.concatenate([k0_ref[:, sl], k1_ref[:, sl], k2_ref[:, sl]], axis=0)
        vp = jnp.concatenate([v0_ref[:, sl], v1_ref[:, sl], v2_ref[:, sl]], axis=0)
        zero = jnp.zeros_like(qp)
        q2 = jnp.concatenate([jnp.where(first, qp, zero), jnp.where(first, zero, qp)], axis=0)
        s = lax.dot_general(q2, kp, (((1,), (1,)), ((), ())), preferred_element_type=F32)
        s = s + bias_ref[p]
        e = jnp.exp2(s - jnp.max(s, axis=-1, keepdims=True))
        l = jnp.sum(e, axis=-1, keepdims=True)
        o2 = jnp.dot(e.astype(BF16), vp, preferred_element_type=F32) / l
        o_ref[:, sl] = jnp.where(first, o2[:TQ], o2[TQ:]).astype(BF16)


def _attention(proj, bias, batch, seq):
    nq = seq // TQ

    def blk(col, back):
        return pl.BlockSpec((TQ, COL_BLK), lambda b, i: (b * nq + jnp.maximum(i - back, 0), col))

    return pl.pallas_call(
        _attn_kernel,
        grid=(batch, nq),
        in_specs=[blk(Q_BLK, 0), blk(K_BLK, 2), blk(K_BLK, 1), blk(K_BLK, 0),
                  blk(V_BLK, 2), blk(V_BLK, 1), blk(V_BLK, 0),
                  pl.BlockSpec((None, A_HEADS // 2, 2 * TQ, 3 * TQ), lambda b, i: (jnp.minimum(i, 2), 0, 0, 0))],
        out_specs=pl.BlockSpec((TQ, A_WIDTH), lambda b, i: (b * nq + i, 0)),
        out_shape=jax.ShapeDtypeStruct((batch * seq, A_WIDTH), BF16),
        compiler_params=_cparams(("parallel", "parallel")),
        name="attention",
    )(proj, proj, proj, proj, proj, proj, proj, bias)


def _attn_bias_table(rel_bias):
    heads = rel_bias.shape[0]
    rb = rel_bias.astype(F32) * LOG2E
    width = 4 * TQ
    n_head = 3 * TQ - 1 - REL_CLIP
    n_tail = width - n_head - (2 * REL_CLIP + 1)
    diag = jnp.concatenate([jnp.broadcast_to(rb[:, -1:], (heads, n_head)), rb[:, ::-1],
                            jnp.broadcast_to(rb[:, :1], (heads, n_tail))], axis=1)
    rolled = jnp.roll(diag, -(TQ - 1), axis=1)
    flat = jnp.broadcast_to(rolled[:, None, :], (heads, TQ, width)).reshape(heads, TQ * width)
    toe = flat[:, :TQ * (width - 1)].reshape(heads, TQ, width - 1)[:, :, :3 * TQ]
    q = jnp.arange(TQ)
    k = jnp.arange(3 * TQ)
    qc = (q // CHUNK)[:, None]
    kc = (k // CHUNK)[None, :]
    band = (kc >= qc) & (kc <= qc + A_LEFT_CHUNKS)
    variants = []
    for v in range(3):
        live = band & (k[None, :] >= (2 - v) * TQ)
        variants.append(jnp.where(live[None], toe, NEG))
    return jnp.stack(variants).reshape(3, heads // 2, 2 * TQ, 3 * TQ)


def _route(x_new, g2, wr_ref, br_ref, carry_ref):
    rows = x_new.shape[0]
    xn = _rms(x_new, g2)
    x_hi = xn.astype(BF16)
    x_lo = (xn - x_hi.astype(F32)).astype(BF16)
    p_hi = jnp.dot(x_hi, wr_ref[...], preferred_element_type=F32)
    p_lo = jnp.dot(x_lo, wr_ref[...], preferred_element_type=F32)
    lg = p_hi + pltpu.roll(p_hi, LANES - 32, 1) + p_lo + br_ref[...]
    lane = lax.broadcasted_iota(jnp.int32, (rows, LANES), 1).astype(F32)

    def top(mask):
        best = jnp.max(jnp.where(mask, lg, -jnp.inf), axis=-1, keepdims=True)
        idx = jnp.min(jnp.where(mask & (lg == best), lane, float(LANES)), axis=-1, keepdims=True)
        return best, idx

    gmask = lane < N_GROUPS
    gmax, grp = top(gmask)
    g_w = 1.0 / jnp.sum(jnp.where(gmask, jnp.exp(lg - gmax), 0.0), axis=-1, keepdims=True)
    lo = N_GROUPS + EXPERTS_PER_GROUP * grp
    in_grp = (lane >= lo) & (lane < lo + EXPERTS_PER_GROUP)
    l1, i1 = top(in_grp)
    l2, i2 = top(in_grp & (lane != i1))
    e2 = jnp.exp(l2 - l1)
    w1 = g_w / (1.0 + e2)
    w2 = g_w * e2 / (1.0 + e2)
    first_is_low = i1 < i2
    cw_a = jnp.where(first_is_low, w1, w2)
    cw_b = jnp.where(first_is_low, w2, w1)
    la = jnp.minimum(i1, i2) - lo
    lb = jnp.maximum(i1, i2) - lo
    bucket = grp * 6.0 + la * (7.0 - la) * 0.5 + (lb - la - 1.0)

    onehot = lane == bucket
    strict = (lax.broadcasted_iota(jnp.int32, (rows, rows), 0) > lax.broadcasted_iota(jnp.int32, (rows, rows), 1))
    prefix = jnp.dot(jnp.where(strict, 1.0, 0.0).astype(BF16), jnp.where(onehot, 1.0, 0.0).astype(BF16),
                     preferred_element_type=F32)
    carry = carry_ref[...]
    rank = jnp.sum(jnp.where(onehot, prefix + carry, 0.0), axis=-1, keepdims=True)
    carry = carry + jnp.sum(jnp.where(onehot, 1.0, 0.0), axis=0, keepdims=True)
    meta = jnp.where(lane == 0.0, bucket, jnp.where(lane == 1.0, rank, jnp.where(lane == 2.0, cw_a,
                     jnp.where(lane == 3.0, cw_b, 0.0))))
    return meta, carry


def _scan_linear(a, b, h_prev):
    n, c = a.shape
    groups = n // 8
    a = a.reshape(groups, 8, c)
    b = b.reshape(groups, 8, c)
    sub = lax.broadcasted_iota(jnp.int32, a.shape, 1)
    for s in (1, 2, 4):
        keep = sub >= s
        b = a * jnp.where(keep, pltpu.roll(b, s, 1), 0.0) + b
        a = a * jnp.where(keep, pltpu.roll(a, s, 1), 1.0)
    outs = []
    for g in range(groups):
        hg = a[g] * h_prev + b[g]
        outs.append(hg)
        h_prev = hg[7:8, :]
    return jnp.concatenate(outs, axis=0), h_prev


def _mix_kernel(gl_ref, u_ref, sv_ref, xg_ref, xc_ref, att_ref, x_ref,
                sgw_ref, sgb_ref, wax_ref, bax_ref, lam_ref,
                wbr_ref, wout_ref, g2_ref, wr_ref, br_ref,
                o_ref, meta_ref, cnt_ref, h_ref, carry_ref):
    t = pl.program_id(1)

    @pl.when((pl.program_id(0) == 0) & (t == 0))
    def _():
        carry_ref[...] = jnp.zeros_like(carry_ref)

    @pl.when(t == 0)
    def _():
        h_ref[...] = jnp.zeros_like(h_ref)

    u = u_ref[...].astype(F32)
    v = sv_ref[...]
    tri = (lax.broadcasted_iota(jnp.int32, (B_BLOCK, B_BLOCK), 0)
           >= lax.broadcasted_iota(jnp.int32, (B_BLOCK, B_BLOCK), 1))
    blocks = []
    for blk in range(TM_MIX // B_BLOCK):
        rs = slice(blk * B_BLOCK, (blk + 1) * B_BLOCK)
        groups = []
        for g in range(B_GROUPS):
            cs = slice(g * LANES, (g + 1) * LANES)
            w = jnp.where(tri, sgw_ref[g], 0.0).astype(BF16)
            groups.append(jnp.dot(w, v[rs, cs], preferred_element_type=F32))
        blocks.append(jnp.concatenate(groups, axis=1) + sgb_ref[...])
    b_out = (u * jnp.concatenate(blocks, axis=0)).astype(BF16)

    xc_bf = xc_ref[...]
    xc = xc_bf.astype(F32)
    ri = jnp.dot(xc_bf, wax_ref[...], preferred_element_type=F32) + bax_ref[...]
    r = _sigmoid(ri[:, :C_WIDTH])
    ig = _sigmoid(ri[:, C_WIDTH:])
    z = -lam_ref[...]
    softplus = jnp.maximum(z, 0.0) + jnp.log1p(jnp.exp(-jnp.abs(z)))
    log_a = -LRU_C * r * softplus
    a = jnp.exp(log_a)
    th = jnp.tanh(log_a)
    mult = jnp.sqrt(jnp.maximum(-2.0 * th / (1.0 - th), 0.0))
    h, h_ref[...] = _scan_linear(a, mult * (ig * xc), h_ref[...])
    c_out = (xg_ref[...].astype(F32) * h).astype(BF16)

    merged = None
    for k, br in enumerate((att_ref[...], b_out, c_out)):
        cs = slice(k * D_MODEL, (k + 1) * D_MODEL)
        term = gl_ref[:, cs].astype(F32) * jnp.dot(br, wbr_ref[k], preferred_element_type=F32)
        merged = term if merged is None else merged + term
    x_new = x_ref[...] + jnp.dot(merged.astype(BF16), wout_ref[...], preferred_element_type=F32)
    o_ref[...] = x_new

    meta, carry = _route(x_new, g2_ref[...], wr_ref, br_ref, carry_ref)
    carry_ref[...] = carry
    meta_ref[...] = meta
    cnt_ref[...] = jnp.broadcast_to(carry, cnt_ref.shape)


def _mix(proj, att, x, p, batch, seq):
    nt = seq // TM_MIX
    n = batch * seq

    def row(width, col):
        return pl.BlockSpec((TM_MIX, width), lambda b, t: (b * nt + t, col))

    def const(shape):
        return pl.BlockSpec(shape, lambda b, t: (0,) * len(shape))

    return pl.pallas_call(
        _mix_kernel,
        grid=(batch, nt),
        in_specs=[row(GATE_COLS, 0), row(COL_BLK, U_BLK), row(COL_BLK, SV_BLK), row(COL_BLK, XG_BLK),
                  row(COL_BLK, XR_BLK), row(A_WIDTH, 0), row(D_MODEL, 0),
                  const((B_GROUPS, B_BLOCK, B_BLOCK)), const((B_BLOCK, B_WIDTH)),
                  const((C_WIDTH, 2 * C_WIDTH)), const((1, 2 * C_WIDTH)), const((1, C_WIDTH)),
                  const((N_BRANCH, A_WIDTH, D_MODEL)), const((D_MODEL, D_MODEL)),
                  const((1, D_MODEL)), const((D_MODEL, LANES)), const((1, LANES))],
        out_specs=[row(D_MODEL, 0), row(LANES, 0), const((8, LANES))],
        out_shape=[jax.ShapeDtypeStruct((n, D_MODEL), F32),
                   jax.ShapeDtypeStruct((n, LANES), F32),
                   jax.ShapeDtypeStruct((8, LANES), F32)],
        scratch_shapes=[pltpu.VMEM((1, C_WIDTH), F32), pltpu.VMEM((1, LANES), F32)],
        compiler_params=_cparams(("arbitrary", "arbitrary")),
        name="mix",
    )(proj, proj, proj, proj, proj, att, x,
      p["sgu_w"], p["sgu_b"], p["w_ax"], p["b_ax"], p["lam"], p["w_branch"], p["w_out"], p["norm2_g"], p["w_r"], p["b_r"])


def _dispatch_kernel(d_ref, x_ref, g_ref, meta_ref, xs_in_ref, xs_ref, rows_ref, sem):
    del xs_in_ref
    rows_ref[:, :D_MODEL] = _rms(x_ref[...], g_ref[...])
    rows_ref[:, D_MODEL:] = meta_ref[...]

    for r in range(TT):
        _row_copy(rows_ref, r, xs_ref, d_ref[0, 0, r], sem).start()
    pltpu.make_async_copy(rows_ref, xs_ref.at[pl.ds(0, TT)], sem).wait()


def _dispatch(dest, x, g, meta, xs_init):
    n = x.shape[0]
    return pl.pallas_call(
        _dispatch_kernel,
        grid=(n // TT,),
        in_specs=[pl.BlockSpec((1, 1, TT), lambda i: (i, 0, 0), memory_space=pltpu.SMEM),
                  pl.BlockSpec((TT, D_MODEL), lambda i: (i, 0)),
                  pl.BlockSpec((1, D_MODEL), lambda i: (0, 0)),
                  pl.BlockSpec((TT, LANES), lambda i: (i, 0)),
                  pl.BlockSpec(memory_space=pl.ANY)],
        out_specs=pl.BlockSpec(memory_space=pl.ANY),
        out_shape=jax.ShapeDtypeStruct(xs_init.shape, F32),
        scratch_shapes=[pltpu.VMEM((TT, ROW_W), F32), pltpu.SemaphoreType.DMA(())],
        input_output_aliases={4: 0},
        compiler_params=_cparams(("arbitrary",)),
        name="dispatch",
    )(dest, x, g, meta, xs_init)


def _combine_kernel(d_ref, x_ref, ys_ref, g_ref, o_ref, buf_ref, sem):
    for r in range(TT):
        _row_copy(ys_ref, d_ref[0, 0, r], buf_ref, r, sem).start()
    pltpu.make_async_copy(ys_ref.at[pl.ds(0, TT)], buf_ref, sem).wait()
    o_ref[...] = _rms(x_ref[...] + buf_ref[...], g_ref[...])


def _combine(dest, x, ys, g):
    n = x.shape[0]
    return pl.pallas_call(
        _combine_kernel,
        grid=(n // TT,),
        in_specs=[pl.BlockSpec((1, 1, TT), lambda i: (i, 0, 0), memory_space=pltpu.SMEM),
                  pl.BlockSpec((TT, D_MODEL), lambda i: (i, 0)),
                  pl.BlockSpec(memory_space=pl.ANY),
                  pl.BlockSpec((1, D_MODEL), lambda i: (0, 0))],
        out_specs=pl.BlockSpec((TT, D_MODEL), lambda i: (i, 0)),
        out_shape=jax.ShapeDtypeStruct((n, D_MODEL), F32),
        scratch_shapes=[pltpu.VMEM((TT, D_MODEL), F32), pltpu.SemaphoreType.DMA(())],
        compiler_params=_cparams(("arbitrary",)),
        name="combine",
    )(dest, x, ys, g)


def _expert_kernel(ta_ref, tb_ref, tv_ref, xs_ref, wgu_a_ref, wd_a_ref, wgu_b_ref, wd_b_ref, o_ref):
    del ta_ref, tb_ref
    t = pl.program_id(0)

    @pl.when(tv_ref[t] == 0)
    def _():
        o_ref[...] = jnp.zeros_like(o_ref)

    @pl.when(tv_ref[t] != 0)
    def _():
        xb = xs_ref[:, :D_MODEL].astype(BF16)
        cw = xs_ref[:, D_MODEL:]

        def expert(wgu_ref, wd_ref):
            gu = jnp.dot(xb, wgu_ref[...], preferred_element_type=F32)
            hid = gu[:, :D_EXPERT] * _sigmoid(gu[:, :D_EXPERT]) * gu[:, D_EXPERT:]
            return jnp.dot(hid.astype(BF16), wd_ref[...], preferred_element_type=F32)

        o_ref[...] = cw[:, 2:3] * expert(wgu_a_ref, wd_a_ref) + cw[:, 3:4] * expert(wgu_b_ref, wd_b_ref)


def _experts(tile_a, tile_b, tile_v, xs, wgu, wd):
    rows = xs.shape[0]
    grid_spec = pltpu.PrefetchScalarGridSpec(
        num_scalar_prefetch=3,
        grid=(rows // TM_E,),
        in_specs=[pl.BlockSpec((TM_E, ROW_W), lambda t, a, b, v: (t, 0)),
                  pl.BlockSpec((None, D_MODEL, 2 * D_EXPERT), lambda t, a, b, v: (a[t], 0, 0)),
                  pl.BlockSpec((None, D_EXPERT, D_MODEL), lambda t, a, b, v: (a[t], 0, 0)),
                  pl.BlockSpec((None, D_MODEL, 2 * D_EXPERT), lambda t, a, b, v: (b[t], 0, 0)),
                  pl.BlockSpec((None, D_EXPERT, D_MODEL), lambda t, a, b, v: (b[t], 0, 0))],
        out_specs=pl.BlockSpec((TM_E, D_MODEL), lambda t, a, b, v: (t, 0)),
    )
    return pl.pallas_call(
        _expert_kernel,
        grid_spec=grid_spec,
        out_shape=jax.ShapeDtypeStruct((rows, D_MODEL), F32),
        compiler_params=_cparams(("arbitrary",)),
        name="experts",
    )(tile_a, tile_b, tile_v, xs, wgu, wd, wgu, wd)


def _plan(meta, counts, n_rows):
    bucket = meta[:, 0].astype(jnp.int32)
    rank = meta[:, 1].astype(jnp.int32)
    cnt = counts[0, :N_BUCKETS].astype(jnp.int32)
    padded = ((cnt + TM_E - 1) // TM_E) * TM_E
    ends = jnp.cumsum(padded)
    base = ends - padded
    ids = jnp.arange(N_BUCKETS, dtype=jnp.int32)
    dest = jnp.sum(jnp.where(bucket[:, None] == ids[None, :], base[None, :], 0), axis=1) + rank
    starts = jnp.arange(n_rows // TM_E, dtype=jnp.int32) * TM_E
    valid = starts < ends[-1]
    probe = jnp.minimum(starts, ends[-1] - 1)
    tile_bucket = jnp.minimum(jnp.sum((probe[:, None] >= ends[None, :]).astype(jnp.int32), axis=1), N_BUCKETS - 1)
    grp = tile_bucket // 6
    pair = tile_bucket % 6
    ge3 = (pair >= 3).astype(jnp.int32)
    ge5 = (pair >= 5).astype(jnp.int32)
    tile_a = grp * EXPERTS_PER_GROUP + ge3 + ge5
    tile_b = grp * EXPERTS_PER_GROUP + pair + 1 - 2 * ge3 - ge5
    return dest, tile_a, tile_b, valid.astype(jnp.int32)


def _block_diag(w):
    nb, d, _ = w.shape
    return jnp.einsum("hij,hg->higj", w, jnp.eye(nb, dtype=w.dtype)).reshape(nb * d, nb * d)


def kernel(x, norm1_g, w_in, b_gate, rel_bias, sgu_norm_g, sgu_w, sgu_b, conv_w, conv_b, rg_wa, rg_ba, rg_wx,
           rg_bx, rg_lambda, w_branch, w_out, norm2_g, router_grp_w, router_grp_b, router_exp_w, router_exp_b,
           exp_w_gate, exp_w_up, exp_w_down, final_norm_g):
    batch, seq, _ = x.shape
    depth = w_in.shape[0]
    n = batch * seq
    assert seq % TQ == 0 and seq % TM_MIX == 0 and n % TT == 0 and n % TM_IN == 0
    n_rows = n + N_BUCKETS * TM_E
    qkv_etc = 3 * A_WIDTH + 2 * B_WIDTH + 2 * C_WIDTH

    q_scale = (A_HEAD_DIM ** -0.5) * LOG2E
    w_in_r = jnp.concatenate([0.5 * w_in[:, :, qkv_etc:], q_scale * w_in[:, :, :A_WIDTH],
                              w_in[:, :, A_WIDTH:qkv_etc]], axis=2).astype(BF16)
    wgu = jnp.concatenate([exp_w_gate, exp_w_up], axis=-1).astype(BF16)
    wd = exp_w_down.astype(BF16)
    w_br = w_branch.astype(BF16)
    w_o = (0.5 * w_out).astype(BF16)
    n_logits = N_GROUPS + N_EXPERTS
    w_r32 = jnp.concatenate([router_grp_w, router_exp_w], axis=-1)
    w_r_hi = w_r32.astype(BF16)
    w_r_lo = (w_r32 - w_r_hi.astype(F32)).astype(BF16)
    zpad = lambda w, k: jnp.pad(w, ((0, 0), (0, 0), (0, k)))
    w_r = jnp.concatenate([zpad(w_r_hi, 32 - n_logits), zpad(w_r_lo, LANES - 32 - n_logits)], axis=-1)
    b_r = jnp.pad(jnp.concatenate([router_grp_b, router_exp_b], axis=-1), ((0, 0), (0, LANES - n_logits)))

    x2 = x.reshape(n, D_MODEL)
    xs = jnp.zeros((n_rows, ROW_W), F32)
    dest3 = ys = None
    for l in range(depth):
        p = {
            "norm1_g": norm1_g[l].reshape(1, D_MODEL),
            "b_gate_half": 0.5 * b_gate[l].reshape(1, GATE_COLS),
            "w_in": w_in_r[l],
            "sgu_norm_g": sgu_norm_g[l].reshape(1, B_WIDTH),
            "sgu_w": sgu_w[l],
            "sgu_b": jnp.repeat(sgu_b[l].T, LANES, axis=1),
            "conv_w": conv_w[l],
            "conv_b": conv_b[l].reshape(1, C_WIDTH),
            "w_ax": jnp.concatenate([_block_diag(rg_wa[l]), _block_diag(rg_wx[l])], axis=1).astype(BF16),
            "b_ax": jnp.concatenate([rg_ba[l], rg_bx[l]]).reshape(1, 2 * C_WIDTH),
            "lam": rg_lambda[l].reshape(1, C_WIDTH),
            "w_branch": w_br[l],
            "w_out": w_o[l],
            "norm2_g": norm2_g[l].reshape(1, D_MODEL),
            "w_r": w_r[l],
            "b_r": b_r[l].reshape(1, LANES),
        }
        x2, proj = _in_proj(x2, p, seq, dest3, ys)
        att = _attention(proj, _attn_bias_table(rel_bias[l]), batch, seq)
        x2, meta, counts = _mix(proj, att, x2, p, batch, seq)

        dest, tile_a, tile_b, tile_v = _plan(meta, counts, n_rows)
        dest3 = dest.reshape(n // TT, 1, TT)
        xs = _dispatch(dest3, x2, p["norm2_g"], meta, xs)
        ys = _experts(tile_a, tile_b, tile_v, xs, wgu[l], wd[l])
    x2 = _combine(dest3, x2, ys, final_norm_g.reshape(1, D_MODEL))
    return x2.reshape(batch, seq, D_MODEL)
```

```python
import functools

import jax
import jax.numpy as jnp
from jax import lax
from jax.experimental import pallas as pl
from jax.experimental.pallas import tpu as pltpu

F32 = jnp.float32
BF16 = jnp.bfloat16

D_MODEL = 1024
CHUNK = 64
A_HEADS = 8
A_HEAD_DIM = 64
A_WIDTH = 512
A_LEFT_CHUNKS = 8
REL_CLIP = 128
B_BLOCK = 128
B_GROUPS = 4
B_WIDTH = 512
C_WIDTH = 512
C_BLOCKS = 8
CONV_WIDTH = 4
LRU_C = 8.0
N_BRANCH = 3
GATE_COLS = N_BRANCH * D_MODEL
IN_COLS = 3 * A_WIDTH + 2 * B_WIDTH + 2 * C_WIDTH + GATE_COLS
N_GROUPS = 4
EXPERTS_PER_GROUP = 4
N_EXPERTS = 16
D_EXPERT = 512
EPS = 1e-6
NEG = -1e30
LOG2E = 1.4426950408889634

COL_BLK = 512
Q_BLK, K_BLK, V_BLK, U_BLK, SV_BLK, XG_BLK, XR_BLK = 6, 7, 8, 9, 10, 11, 12

TM_IN = 512
TQ = 256
TM_MIX = 256
TT = 512
TM_E = 256
N_BUCKETS = N_GROUPS * 6
LANES = 128
ROW_W = D_MODEL + LANES
VMEM_LIMIT = 56 * 1024 * 1024


def _cparams(sem):
    return pltpu.CompilerParams(dimension_semantics=sem, vmem_limit_bytes=VMEM_LIMIT)


def _rms(x, g):
    return x * lax.rsqrt(jnp.mean(x * x, axis=-1, keepdims=True) + EPS) * g


def _sigmoid(x):
    return 0.5 * jnp.tanh(0.5 * x) + 0.5


def _row_copy(src_ref, src_row, dst_ref, dst_row, sem):
    return pltpu.make_async_copy(src_ref.at[pl.ds(src_row, 1)], dst_ref.at[pl.ds(dst_row, 1)], sem)


def _gelu(x):
    c = 0.7978845608028654
    h = 0.5 * x
    return h + h * jnp.tanh(x * (c + (c * 0.044715) * (x * x)))


def _project(x, g_ref, bg_ref, sgn_ref, cw_ref, cb_ref, w_ref, o_ref, halo_ref, between=None):
    xn = _rms(x, g_ref[...]).astype(BF16)
    heavy = [XR_BLK, SV_BLK, U_BLK, XG_BLK]
    order = heavy + [j for j in range(IN_COLS // COL_BLK) if j not in heavy]
    for step, j in enumerate(order):
        sl = slice(j * COL_BLK, (j + 1) * COL_BLK)
        acc = jnp.dot(xn, w_ref[:, sl], preferred_element_type=F32)
        if j < GATE_COLS // COL_BLK:
            acc = 1.0 + jnp.tanh(acc + bg_ref[:, sl])
        elif j in (U_BLK, XG_BLK):
            acc = _gelu(acc)
        elif j == SV_BLK:
            acc = _rms(_gelu(acc), sgn_ref[...])
        elif j == XR_BLK:
            ext = jnp.concatenate([halo_ref[...], acc], axis=0)
            halo_ref[...] = acc[TM_IN - 8:, :]
            conv = cb_ref[...]
            for k in range(CONV_WIDTH):
                lo = 8 - (CONV_WIDTH - 1) + k
                conv = conv + cw_ref[k:k + 1, :] * ext[lo:lo + TM_IN, :]
            acc = conv
        o_ref[:, sl] = acc.astype(BF16)
        if between is not None:
            between(step)


def _reset_halo(halo_ref, tiles_per_seq):
    @pl.when(pl.program_id(0) % tiles_per_seq == 0)
    def _():
        halo_ref[...] = jnp.zeros_like(halo_ref)


def _inproj_kernel(tiles_per_seq, x_ref, g_ref, bg_ref, sgn_ref, cw_ref, cb_ref, w_ref, o_ref, halo_ref):
    _reset_halo(halo_ref, tiles_per_seq)
    _project(x_ref[...], g_ref, bg_ref, sgn_ref, cw_ref, cb_ref, w_ref, o_ref, halo_ref)


def _inproj_combine_kernel(tiles_per_seq, dcur_ref, dnext_ref, x_ref, ys_ref, g_ref, bg_ref, sgn_ref, cw_ref, cb_ref,
                           w_ref, xo_ref, o_ref, buf_ref, halo_ref, sem):
    i = pl.program_id(0)
    slot = i % 2
    nxt = 1 - slot

    def gather(d_ref, s, r):
        return _row_copy(ys_ref, d_ref[0, 0, r], buf_ref.at[s], r, sem.at[s])

    def wait_rows(s):
        pltpu.make_async_copy(ys_ref.at[pl.ds(0, TM_IN)], buf_ref.at[s], sem.at[s]).wait()

    @pl.when(i == 0)
    def _():
        def body(r, c):
            gather(dcur_ref, 0, r).start()
            return c
        lax.fori_loop(0, TM_IN, body, 0, unroll=8)

    wait_rows(slot)
    x_new = x_ref[...] + buf_ref[slot]
    xo_ref[...] = x_new

    n_chunks = IN_COLS // COL_BLK
    per = -(-TM_IN // n_chunks)

    def issue(j):
        for r in range(j * per, min((j + 1) * per, TM_IN)):
            gather(dnext_ref, nxt, r).start()

    _reset_halo(halo_ref, tiles_per_seq)
    _project(x_new, g_ref, bg_ref, sgn_ref, cw_ref, cb_ref, w_ref, o_ref, halo_ref, between=issue)

    @pl.when(i == pl.num_programs(0) - 1)
    def _():
        wait_rows(nxt)


def _in_proj(x, p, seq, dest3=None, ys=None):
    n = x.shape[0]
    nt = n // TM_IN
    tiles_per_seq = seq // TM_IN

    def const(shape, **kw):
        return pl.BlockSpec(shape, lambda i: (0,) * len(shape), **kw)

    x_spec = pl.BlockSpec((TM_IN, D_MODEL), lambda i: (i, 0))
    params = [p["norm1_g"], p["b_gate_half"], p["sgu_norm_g"], p["conv_w"], p["conv_b"], p["w_in"]]
    param_specs = [const((1, D_MODEL)), const((1, GATE_COLS)), const((1, B_WIDTH)), const((CONV_WIDTH, C_WIDTH)),
                   const((1, C_WIDTH)), const((D_MODEL, IN_COLS), pipeline_mode=pl.Buffered(1))]
    o_spec = pl.BlockSpec((TM_IN, IN_COLS), lambda i: (i, 0))
    o_shape = jax.ShapeDtypeStruct((n, IN_COLS), BF16)
    halo = pltpu.VMEM((8, C_WIDTH), F32)
    if ys is None:
        return x, pl.pallas_call(
            functools.partial(_inproj_kernel, tiles_per_seq), grid=(nt,), in_specs=[x_spec] + param_specs,
            out_specs=o_spec, out_shape=o_shape, scratch_shapes=[halo],
            compiler_params=_cparams(("arbitrary",)), name="in_proj",
        )(x, *params)
    assert TM_IN == TT
    return pl.pallas_call(
        functools.partial(_inproj_combine_kernel, tiles_per_seq),
        grid=(nt,),
        in_specs=[pl.BlockSpec((1, 1, TT), lambda i: (i, 0, 0), memory_space=pltpu.SMEM),
                  pl.BlockSpec((1, 1, TT), lambda i: (jnp.minimum(i + 1, nt - 1), 0, 0), memory_space=pltpu.SMEM),
                  x_spec, pl.BlockSpec(memory_space=pl.ANY)] + param_specs,
        out_specs=[x_spec, o_spec],
        out_shape=[jax.ShapeDtypeStruct((n, D_MODEL), F32), o_shape],
        scratch_shapes=[pltpu.VMEM((2, TM_IN, D_MODEL), F32), halo, pltpu.SemaphoreType.DMA((2,))],
        compiler_params=_cparams(("arbitrary",)),
        name="in_proj_combine",
    )(dest3, dest3, x, ys, *params)


def _attn_kernel(q_ref, k0_ref, k1_ref, k2_ref, k3_ref, v0_ref, v1_ref, v2_ref, v3_ref, bias0_ref, bias1_ref, o_ref):
    lane = lax.broadcasted_iota(jnp.int32, (TQ, LANES), 1)
    first = lane < A_HEAD_DIM
    k_refs = (k0_ref, k1_ref, k2_ref, k3_ref)
    v_refs = (v0_ref, v1_ref, v2_ref, v3_ref)
    for j, bias_ref in enumerate((bias0_ref, bias1_ref)):
        rows = slice(j * TQ, (j + 1) * TQ)
        for p in range(A_HEADS // 2):
            sl = slice(p * LANES, (p + 1) * LANES)
            qp = q_ref[rows, sl]
            kp = jnp.concatenate([r[:, sl] for r in k_refs[j:j + 3]], axis=0)
            vp = jnp.concatenate([r[:, sl] for r in v_refs[j:j + 3]], axis=0)
            zero = jnp.zeros_like(qp)
            q2 = jnp.concatenate([jnp.where(first, qp, zero), jnp.where(first, zero, qp)], axis=0)
            s = lax.dot_general(q2, kp, (((1,), (1,)), ((), ())), preferred_element_type=F32)
            s = s + bias_ref[p]
            e = jnp.exp2(s - jnp.max(s, axis=-1, keepdims=True))
            l = jnp.sum(e, axis=-1, keepdims=True)
            o2 = jnp.dot(e.astype(BF16), vp, preferred_element_type=F32) / l
            o_ref[rows, sl] = jnp.where(first, o2[:TQ], o2[TQ:]).astype(BF16)


def _attention(proj, bias, batch, seq):
    nq = seq // TQ
    nstep = nq // 2

    def q_blk(col):
        return pl.BlockSpec((2 * TQ, COL_BLK), lambda b, i: (b * nstep + i, col))

    def kv_blk(col, off):
        return pl.BlockSpec((TQ, COL_BLK), lambda b, i: (b * nq + jnp.maximum(2 * i + off, 0), col))

    def bias_blk(j):
        return pl.BlockSpec((None, A_HEADS // 2, 2 * TQ, 3 * TQ), lambda b, i: (jnp.minimum(2 * i + j, 2), 0, 0, 0))

    return pl.pallas_call(
        _attn_kernel,
        grid=(batch, nstep),
        in_specs=[q_blk(Q_BLK)] + [kv_blk(K_BLK, off) for off in (-2, -1, 0, 1)]
        + [kv_blk(V_BLK, off) for off in (-2, -1, 0, 1)] + [bias_blk(0), bias_blk(1)],
        out_specs=pl.BlockSpec((2 * TQ, A_WIDTH), lambda b, i: (b * nstep + i, 0)),
        out_shape=jax.ShapeDtypeStruct((batch * seq, A_WIDTH), BF16),
        compiler_params=_cparams(("parallel", "parallel")),
        name="attention",
    )(*([proj] * 9), bias, bias)


def _attn_bias_table(rel_bias):
    heads = rel_bias.shape[0]
    rb = rel_bias.astype(F32) * LOG2E
    width = 4 * TQ
    n_head = 3 * TQ - 1 - REL_CLIP
    n_tail = width - n_head - (2 * REL_CLIP + 1)
    diag = jnp.concatenate([jnp.broadcast_to(rb[:, -1:], (heads, n_head)), rb[:, ::-1],
                            jnp.broadcast_to(rb[:, :1], (heads, n_tail))], axis=1)
    rolled = jnp.roll(diag, -(TQ - 1), axis=1)
    flat = jnp.broadcast_to(rolled[:, None, :], (heads, TQ, width)).reshape(heads, TQ * width)
    toe = flat[:, :TQ * (width - 1)].reshape(heads, TQ, width - 1)[:, :, :3 * TQ]
    q = jnp.arange(TQ)
    k = jnp.arange(3 * TQ)
    qc = (q // CHUNK)[:, None]
    kc = (k // CHUNK)[None, :]
    band = (kc >= qc) & (kc <= qc + A_LEFT_CHUNKS)
    variants = []
    for v in range(3):
        live = band & (k[None, :] >= (2 - v) * TQ)
        variants.append(jnp.where(live[None], toe, NEG))
    return jnp.stack(variants).reshape(3, heads // 2, 2 * TQ, 3 * TQ)


def _route(x_new, g2, wr_ref, br_ref, carry_ref):
    rows = x_new.shape[0]
    xn = _rms(x_new, g2)
    x_hi = xn.astype(BF16)
    x_lo = (xn - x_hi.astype(F32)).astype(BF16)
    p_hi = jnp.dot(x_hi, wr_ref[...], preferred_element_type=F32)
    p_lo = jnp.dot(x_lo, wr_ref[...], preferred_element_type=F32)
    lg = p_hi + pltpu.roll(p_hi, LANES - 32, 1) + p_lo + br_ref[...]
    lane = lax.broadcasted_iota(jnp.int32, (rows, LANES), 1).astype(F32)

    def top(mask):
        best = jnp.max(jnp.where(mask, lg, -jnp.inf), axis=-1, keepdims=True)
        idx = jnp.min(jnp.where(mask & (lg == best), lane, float(LANES)), axis=-1, keepdims=True)
        return best, idx

    gmask = lane < N_GROUPS
    gmax, grp = top(gmask)
    g_w = 1.0 / jnp.sum(jnp.where(gmask, jnp.exp(lg - gmax), 0.0), axis=-1, keepdims=True)
    lo = N_GROUPS + EXPERTS_PER_GROUP * grp
    in_grp = (lane >= lo) & (lane < lo + EXPERTS_PER_GROUP)
    l1, i1 = top(in_grp)
    l2, i2 = top(in_grp & (lane != i1))
    e2 = jnp.exp(l2 - l1)
    w1 = g_w / (1.0 + e2)
    w2 = g_w * e2 / (1.0 + e2)
    first_is_low = i1 < i2
    cw_a = jnp.where(first_is_low, w1, w2)
    cw_b = jnp.where(first_is_low, w2, w1)
    la = jnp.minimum(i1, i2) - lo
    lb = jnp.maximum(i1, i2) - lo
    bucket = grp * 6.0 + la * (7.0 - la) * 0.5 + (lb - la - 1.0)

    onehot = lane == bucket
    strict = (lax.broadcasted_iota(jnp.int32, (rows, rows), 0) > lax.broadcasted_iota(jnp.int32, (rows, rows), 1))
    prefix = jnp.dot(jnp.where(strict, 1.0, 0.0).astype(BF16), jnp.where(onehot, 1.0, 0.0).astype(BF16),
                     preferred_element_type=F32)
    carry = carry_ref[...]
    rank = jnp.sum(jnp.where(onehot, prefix + carry, 0.0), axis=-1, keepdims=True)
    carry = carry + jnp.sum(jnp.where(onehot, 1.0, 0.0), axis=0, keepdims=True)
    meta = jnp.where(lane == 0.0, bucket, jnp.where(lane == 1.0, rank, jnp.where(lane == 2.0, cw_a,
                     jnp.where(lane == 3.0, cw_b, 0.0))))
    return meta, carry


def _scan_linear(a, b, h_prev):
    n, c = a.shape
    groups = n // 8
    a = a.reshape(groups, 8, c)
    b = b.reshape(groups, 8, c)
    sub = lax.broadcasted_iota(jnp.int32, a.shape, 1)
    for s in (1, 2, 4):
        keep = sub >= s
        b = a * jnp.where(keep, pltpu.roll(b, s, 1), 0.0) + b
        a = a * jnp.where(keep, pltpu.roll(a, s, 1), 1.0)
    outs = []
    for g in range(groups):
        hg = a[g] * h_prev + b[g]
        outs.append(hg)
        h_prev = hg[7:8, :]
    return jnp.concatenate(outs, axis=0), h_prev


def _mix_kernel(gl_ref, u_ref, sv_ref, xg_ref, xc_ref, att_ref, x_ref,
                sgw_ref, sgb_ref, wax_ref, bax_ref, lam_ref,
                wbr_ref, wout_ref, g2_ref, wr_ref, br_ref,
                o_ref, meta_ref, cnt_ref, h_ref, carry_ref):
    t = pl.program_id(1)

    @pl.when((pl.program_id(0) == 0) & (t == 0))
    def _():
        carry_ref[...] = jnp.zeros_like(carry_ref)

    @pl.when(t == 0)
    def _():
        h_ref[...] = jnp.zeros_like(h_ref)

    u = u_ref[...].astype(F32)
    v = sv_ref[...]
    tri = (lax.broadcasted_iota(jnp.int32, (B_BLOCK, B_BLOCK), 0)
           >= lax.broadcasted_iota(jnp.int32, (B_BLOCK, B_BLOCK), 1))
    blocks = []
    for blk in range(TM_MIX // B_BLOCK):
        rs = slice(blk * B_BLOCK, (blk + 1) * B_BLOCK)
        groups = []
        for g in range(B_GROUPS):
            cs = slice(g * LANES, (g + 1) * LANES)
            w = jnp.where(tri, sgw_ref[g], 0.0).astype(BF16)
            groups.append(jnp.dot(w, v[rs, cs], preferred_element_type=F32))
        blocks.append(jnp.concatenate(groups, axis=1) + sgb_ref[...])
    b_out = (u * jnp.concatenate(blocks, axis=0)).astype(BF16)

    xc_bf = xc_ref[...]
    xc = xc_bf.astype(F32)
    ri = jnp.dot(xc_bf, wax_ref[...], preferred_element_type=F32) + bax_ref[...]
    r = _sigmoid(ri[:, :C_WIDTH])
    ig = _sigmoid(ri[:, C_WIDTH:])
    z = -lam_ref[...]
    softplus = jnp.maximum(z, 0.0) + jnp.log1p(jnp.exp(-jnp.abs(z)))
    log_a = -LRU_C * r * softplus
    a = jnp.exp(log_a)
    th = jnp.tanh(log_a)
    mult = jnp.sqrt(jnp.maximum(-2.0 * th / (1.0 - th), 0.0))
    h, h_ref[...] = _scan_linear(a, mult * (ig * xc), h_ref[...])
    c_out = (xg_ref[...].astype(F32) * h).astype(BF16)

    merged = None
    for k, br in enumerate((att_ref[...], b_out, c_out)):
        cs = slice(k * D_MODEL, (k + 1) * D_MODEL)
        term = gl_ref[:, cs] * jnp.dot(br, wbr_ref[k], preferred_element_type=F32).astype(BF16)
        merged = term if merged is None else merged + term
    x_new = x_ref[...] + jnp.dot(merged, wout_ref[...], preferred_element_type=F32)
    o_ref[...] = x_new

    meta, carry = _route(x_new, g2_ref[...], wr_ref, br_ref, carry_ref)
    carry_ref[...] = carry
    meta_ref[...] = meta
    cnt_ref[...] = jnp.broadcast_to(carry, cnt_ref.shape)


def _mix(proj, att, x, p, batch, seq):
    nt = seq // TM_MIX
    n = batch * seq

    def row(width, col):
        return pl.BlockSpec((TM_MIX, width), lambda b, t: (b * nt + t, col))

    def const(shape):
        return pl.BlockSpec(shape, lambda b, t: (0,) * len(shape))

    return pl.pallas_call(
        _mix_kernel,
        grid=(batch, nt),
        in_specs=[row(GATE_COLS, 0), row(COL_BLK, U_BLK), row(COL_BLK, SV_BLK), row(COL_BLK, XG_BLK),
                  row(COL_BLK, XR_BLK), row(A_WIDTH, 0), row(D_MODEL, 0),
                  const((B_GROUPS, B_BLOCK, B_BLOCK)), const((B_BLOCK, B_WIDTH)),
                  const((C_WIDTH, 2 * C_WIDTH)), const((1, 2 * C_WIDTH)), const((1, C_WIDTH)),
                  const((N_BRANCH, A_WIDTH, D_MODEL)), const((D_MODEL, D_MODEL)),
                  const((1, D_MODEL)), const((D_MODEL, LANES)), const((1, LANES))],
        out_specs=[row(D_MODEL, 0), row(LANES, 0), const((8, LANES))],
        out_shape=[jax.ShapeDtypeStruct((n, D_MODEL), F32),
                   jax.ShapeDtypeStruct((n, LANES), F32),
                   jax.ShapeDtypeStruct((8, LANES), F32)],
        scratch_shapes=[pltpu.VMEM((1, C_WIDTH), F32), pltpu.VMEM((1, LANES), F32)],
        compiler_params=_cparams(("arbitrary", "arbitrary")),
        name="mix",
    )(proj, proj, proj, proj, proj, att, x,
      p["sgu_w"], p["sgu_b"], p["w_ax"], p["b_ax"], p["lam"], p["w_branch"], p["w_out"], p["norm2_g"], p["w_r"], p["b_r"])


def _dispatch_kernel(d_ref, x_ref, g_ref, meta_ref, xs_in_ref, xs_ref, rows_ref, sem):
    del xs_in_ref
    rows_ref[:, :D_MODEL] = _rms(x_ref[...], g_ref[...])
    rows_ref[:, D_MODEL:] = meta_ref[...]

    for r in range(TT):
        _row_copy(rows_ref, r, xs_ref, d_ref[0, 0, r], sem).start()
    pltpu.make_async_copy(rows_ref, xs_ref.at[pl.ds(0, TT)], sem).wait()


def _dispatch(dest, x, g, meta, xs_init):
    n = x.shape[0]
    return pl.pallas_call(
        _dispatch_kernel,
        grid=(n // TT,),
        in_specs=[pl.BlockSpec((1, 1, TT), lambda i: (i, 0, 0), memory_space=pltpu.SMEM),
                  pl.BlockSpec((TT, D_MODEL), lambda i: (i, 0)),
                  pl.BlockSpec((1, D_MODEL), lambda i: (0, 0)),
                  pl.BlockSpec((TT, LANES), lambda i: (i, 0)),
                  pl.BlockSpec(memory_space=pl.ANY)],
        out_specs=pl.BlockSpec(memory_space=pl.ANY),
        out_shape=jax.ShapeDtypeStruct(xs_init.shape, F32),
        scratch_shapes=[pltpu.VMEM((TT, ROW_W), F32), pltpu.SemaphoreType.DMA(())],
        input_output_aliases={4: 0},
        compiler_params=_cparams(("arbitrary",)),
        name="dispatch",
    )(dest, x, g, meta, xs_init)


def _combine_kernel(d_ref, x_ref, ys_ref, g_ref, o_ref, buf_ref, sem):
    for r in range(TT):
        _row_copy(ys_ref, d_ref[0, 0, r], buf_ref, r, sem).start()
    pltpu.make_async_copy(ys_ref.at[pl.ds(0, TT)], buf_ref, sem).wait()
    o_ref[...] = _rms(x_ref[...] + buf_ref[...], g_ref[...])


def _combine(dest, x, ys, g):
    n = x.shape[0]
    return pl.pallas_call(
        _combine_kernel,
        grid=(n // TT,),
        in_specs=[pl.BlockSpec((1, 1, TT), lambda i: (i, 0, 0), memory_space=pltpu.SMEM),
                  pl.BlockSpec((TT, D_MODEL), lambda i: (i, 0)),
                  pl.BlockSpec(memory_space=pl.ANY),
                  pl.BlockSpec((1, D_MODEL), lambda i: (0, 0))],
        out_specs=pl.BlockSpec((TT, D_MODEL), lambda i: (i, 0)),
        out_shape=jax.ShapeDtypeStruct((n, D_MODEL), F32),
        scratch_shapes=[pltpu.VMEM((TT, D_MODEL), F32), pltpu.SemaphoreType.DMA(())],
        compiler_params=_cparams(("arbitrary",)),
        name="combine",
    )(dest, x, ys, g)


def _expert_kernel(ta_ref, tb_ref, tv_ref, tn_ref, xs_ref, ga_ref, ua_ref, da_ref, gb_ref, ub_ref, db_ref, o_ref,
                   wgu_ref, wd_ref):
    del ta_ref, tb_ref
    t = pl.program_id(0)

    @pl.when(tn_ref[t] != 0)
    def _():
        for e, (g_ref, u_ref, d_ref) in enumerate(((ga_ref, ua_ref, da_ref), (gb_ref, ub_ref, db_ref))):
            wgu_ref[e, :, :D_EXPERT] = g_ref[...].astype(BF16)
            wgu_ref[e, :, D_EXPERT:] = u_ref[...].astype(BF16)
            wd_ref[e] = d_ref[...].astype(BF16)

    @pl.when(tv_ref[t] == 0)
    def _():
        o_ref[...] = jnp.zeros_like(o_ref)

    @pl.when(tv_ref[t] != 0)
    def _():
        xb = xs_ref[:, :D_MODEL].astype(BF16)
        cw = xs_ref[:, D_MODEL:]

        def expert(e):
            gu = jnp.dot(xb, wgu_ref[e], preferred_element_type=F32)
            hid = gu[:, :D_EXPERT] * _sigmoid(gu[:, :D_EXPERT]) * gu[:, D_EXPERT:]
            return jnp.dot(hid.astype(BF16), wd_ref[e], preferred_element_type=F32)

        o_ref[...] = cw[:, 2:3] * expert(0) + cw[:, 3:4] * expert(1)


def _experts(tile_a, tile_b, tile_v, tile_new, xs, w_gate, w_up, w_down, layer):
    rows = xs.shape[0]

    def w_spec(rows_, cols_, which):
        return pl.BlockSpec((None, None, rows_, cols_),
                            lambda t, a, b, v, n: (layer, (a if which == 0 else b)[t], 0, 0))

    grid_spec = pltpu.PrefetchScalarGridSpec(
        num_scalar_prefetch=4,
        grid=(rows // TM_E,),
        in_specs=[pl.BlockSpec((TM_E, ROW_W), lambda t, a, b, v, n: (t, 0)),
                  w_spec(D_MODEL, D_EXPERT, 0), w_spec(D_MODEL, D_EXPERT, 0), w_spec(D_EXPERT, D_MODEL, 0),
                  w_spec(D_MODEL, D_EXPERT, 1), w_spec(D_MODEL, D_EXPERT, 1), w_spec(D_EXPERT, D_MODEL, 1)],
        out_specs=pl.BlockSpec((TM_E, D_MODEL), lambda t, a, b, v, n: (t, 0)),
        scratch_shapes=[pltpu.VMEM((2, D_MODEL, 2 * D_EXPERT), BF16), pltpu.VMEM((2, D_EXPERT, D_MODEL), BF16)],
    )
    return pl.pallas_call(
        _expert_kernel,
        grid_spec=grid_spec,
        out_shape=jax.ShapeDtypeStruct((rows, D_MODEL), F32),
        compiler_params=_cparams(("arbitrary",)),
        name="experts",
    )(tile_a, tile_b, tile_v, tile_new, xs, w_gate, w_up, w_down, w_gate, w_up, w_down)


def _plan(meta, counts, n_rows):
    bucket = meta[:, 0].astype(jnp.int32)
    rank = meta[:, 1].astype(jnp.int32)
    cnt = counts[0, :N_BUCKETS].astype(jnp.int32)
    padded = ((cnt + TM_E - 1) // TM_E) * TM_E
    ends = jnp.cumsum(padded)
    base = ends - padded
    ids = jnp.arange(N_BUCKETS, dtype=jnp.int32)
    dest = jnp.sum(jnp.where(bucket[:, None] == ids[None, :], base[None, :], 0), axis=1) + rank
    starts = jnp.arange(n_rows // TM_E, dtype=jnp.int32) * TM_E
    valid = starts < ends[-1]
    probe = jnp.minimum(starts, ends[-1] - 1)
    tile_bucket = jnp.minimum(jnp.sum((probe[:, None] >= ends[None, :]).astype(jnp.int32), axis=1), N_BUCKETS - 1)
    grp = tile_bucket // 6
    pair = tile_bucket % 6
    ge3 = (pair >= 3).astype(jnp.int32)
    ge5 = (pair >= 5).astype(jnp.int32)
    tile_a = grp * EXPERTS_PER_GROUP + ge3 + ge5
    tile_b = grp * EXPERTS_PER_GROUP + pair + 1 - 2 * ge3 - ge5
    tile_new = jnp.concatenate([jnp.ones((1,), jnp.int32), (tile_bucket[1:] != tile_bucket[:-1]).astype(jnp.int32)])
    return dest, tile_a, tile_b, valid.astype(jnp.int32), tile_new


def _block_diag(w):
    nb, d, _ = w.shape
    return jnp.einsum("hij,hg->higj", w, jnp.eye(nb, dtype=w.dtype)).reshape(nb * d, nb * d)


def kernel(x, norm1_g, w_in, b_gate, rel_bias, sgu_norm_g, sgu_w, sgu_b, conv_w, conv_b, rg_wa, rg_ba, rg_wx,
           rg_bx, rg_lambda, w_branch, w_out, norm2_g, router_grp_w, router_grp_b, router_exp_w, router_exp_b,
           exp_w_gate, exp_w_up, exp_w_down, final_norm_g):
    batch, seq, _ = x.shape
    depth = w_in.shape[0]
    n = batch * seq
    assert seq % (2 * TQ) == 0 and seq % TM_MIX == 0 and seq % TM_IN == 0 and n % TT == 0
    n_rows = n + N_BUCKETS * TM_E
    qkv_etc = 3 * A_WIDTH + 2 * B_WIDTH + 2 * C_WIDTH

    q_scale = (A_HEAD_DIM ** -0.5) * LOG2E
    w_in_r = jnp.concatenate([0.5 * w_in[:, :, qkv_etc:], q_scale * w_in[:, :, :A_WIDTH],
                              w_in[:, :, A_WIDTH:qkv_etc]], axis=2).astype(BF16)
    w_br = w_branch.astype(BF16)
    w_o = (0.5 * w_out).astype(BF16)
    n_logits = N_GROUPS + N_EXPERTS
    w_r32 = jnp.concatenate([router_grp_w, router_exp_w], axis=-1)
    w_r_hi = w_r32.astype(BF16)
    w_r_lo = (w_r32 - w_r_hi.astype(F32)).astype(BF16)
    zpad = lambda w, k: jnp.pad(w, ((0, 0), (0, 0), (0, k)))
    w_r = jnp.concatenate([zpad(w_r_hi, 32 - n_logits), zpad(w_r_lo, LANES - 32 - n_logits)], axis=-1)
    b_r = jnp.pad(jnp.concatenate([router_grp_b, router_exp_b], axis=-1), ((0, 0), (0, LANES - n_logits)))

    x2 = x.reshape(n, D_MODEL)
    xs = jnp.zeros((n_rows, ROW_W), F32)
    dest3 = ys = None
    for l in range(depth):
        p = {
            "norm1_g": norm1_g[l].reshape(1, D_MODEL),
            "b_gate_half": 0.5 * b_gate[l].reshape(1, GATE_COLS),
            "w_in": w_in_r[l],
            "sgu_norm_g": sgu_norm_g[l].reshape(1, B_WIDTH),
            "sgu_w": sgu_w[l],
            "sgu_b": jnp.repeat(sgu_b[l].T, LANES, axis=1),
            "conv_w": conv_w[l],
            "conv_b": conv_b[l].reshape(1, C_WIDTH),
            "w_ax": jnp.concatenate([_block_diag(rg_wa[l]), _block_diag(rg_wx[l])], axis=1).astype(BF16),
            "b_ax": jnp.concatenate([rg_ba[l], rg_bx[l]]).reshape(1, 2 * C_WIDTH),
            "lam": rg_lambda[l].reshape(1, C_WIDTH),
            "w_branch": w_br[l],
            "w_out": w_o[l],
            "norm2_g": norm2_g[l].reshape(1, D_MODEL),
            "w_r": w_r[l],
            "b_r": b_r[l].reshape(1, LANES),
        }
        x2, proj = _in_proj(x2, p, seq, dest3, ys)
        att = _attention(proj, _attn_bias_table(rel_bias[l]), batch, seq)
        x2, meta, counts = _mix(proj, att, x2, p, batch, seq)

        dest, tile_a, tile_b, tile_v, tile_new = _plan(meta, counts, n_rows)
        dest3 = dest.reshape(n // TT, 1, TT)
        xs = _dispatch(dest3, x2, p["norm2_g"], meta, xs)
        ys = _experts(tile_a, tile_b, tile_v, tile_new, xs, exp_w_gate, exp_w_up, exp_w_down, l)
    x2 = _combine(dest3, x2, ys, final_norm_g.reshape(1, D_MODEL))
    return x2.reshape(batch, seq, D_MODEL)
```

```python
import functools

import jax
import jax.numpy as jnp
from jax import lax
from jax.experimental import pallas as pl
from jax.experimental.pallas import tpu as pltpu

F32 = jnp.float32
BF16 = jnp.bfloat16

D_MODEL = 1024
CHUNK = 64
A_HEADS = 8
A_HEAD_DIM = 64
A_WIDTH = 512
A_LEFT_CHUNKS = 8
REL_CLIP = 128
B_BLOCK = 128
B_GROUPS = 4
B_WIDTH = 512
C_WIDTH = 512
C_BLOCKS = 8
CONV_WIDTH = 4
LRU_C = 8.0
N_BRANCH = 3
GATE_COLS = N_BRANCH * D_MODEL
IN_COLS = 3 * A_WIDTH + 2 * B_WIDTH + 2 * C_WIDTH + GATE_COLS
N_GROUPS = 4
EXPERTS_PER_GROUP = 4
N_EXPERTS = 16
D_EXPERT = 512
EPS = 1e-6
NEG = -1e30
LOG2E = 1.4426950408889634

COL_BLK = 512
Q_BLK, K_BLK, V_BLK, U_BLK, SV_BLK, XG_BLK, XR_BLK = 6, 7, 8, 9, 10, 11, 12

TM_IN = 512
TQ = 256
TM_MIX = 512
TT = 512
TM_E = 256
N_BUCKETS = N_GROUPS * 6
LANES = 128
ROW_W = D_MODEL + LANES
VMEM_LIMIT = 56 * 1024 * 1024


def _cparams(sem):
    return pltpu.CompilerParams(dimension_semantics=sem, vmem_limit_bytes=VMEM_LIMIT)


def _rms(x, g):
    return x * lax.rsqrt(jnp.mean(x * x, axis=-1, keepdims=True) + EPS) * g


def _sigmoid(x):
    return 0.5 * jnp.tanh(0.5 * x) + 0.5


def _row_copy(src_ref, src_row, dst_ref, dst_row, sem):
    return pltpu.make_async_copy(src_ref.at[pl.ds(src_row, 1)], dst_ref.at[pl.ds(dst_row, 1)], sem)


def _gelu(x):
    c = 0.7978845608028654
    h = 0.5 * x
    return h + h * jnp.tanh(x * (c + (c * 0.044715) * (x * x)))


def _project(x, g_ref, bg_ref, sgn_ref, cw_ref, cb_ref, w_ref, o_ref, halo_ref, between=None):
    xn = _rms(x, g_ref[...]).astype(BF16)
    heavy = [XR_BLK, SV_BLK, U_BLK, XG_BLK]
    order = heavy + [j for j in range(IN_COLS // COL_BLK) if j not in heavy]
    for step, j in enumerate(order):
        sl = slice(j * COL_BLK, (j + 1) * COL_BLK)
        acc = jnp.dot(xn, w_ref[:, sl], preferred_element_type=F32)
        if j < GATE_COLS // COL_BLK:
            acc = 1.0 + jnp.tanh(acc + bg_ref[:, sl])
        elif j in (U_BLK, XG_BLK):
            acc = _gelu(acc)
        elif j == SV_BLK:
            acc = _rms(_gelu(acc), sgn_ref[...])
        elif j == XR_BLK:
            ext = jnp.concatenate([halo_ref[...], acc], axis=0)
            halo_ref[...] = acc[TM_IN - 8:, :]
            conv = cb_ref[...]
            for k in range(CONV_WIDTH):
                lo = 8 - (CONV_WIDTH - 1) + k
                conv = conv + cw_ref[k:k + 1, :] * ext[lo:lo + TM_IN, :]
            acc = conv
        o_ref[:, sl] = acc.astype(BF16)
        if between is not None:
            between(step)


def _reset_halo(halo_ref, tiles_per_seq):
    @pl.when(pl.program_id(0) % tiles_per_seq == 0)
    def _():
        halo_ref[...] = jnp.zeros_like(halo_ref)


def _inproj_kernel(tiles_per_seq, x_ref, g_ref, bg_ref, sgn_ref, cw_ref, cb_ref, w_ref, o_ref, halo_ref):
    _reset_halo(halo_ref, tiles_per_seq)
    _project(x_ref[...], g_ref, bg_ref, sgn_ref, cw_ref, cb_ref, w_ref, o_ref, halo_ref)


def _inproj_combine_kernel(tiles_per_seq, dcur_ref, dnext_ref, x_ref, ys_ref, g_ref, bg_ref, sgn_ref, cw_ref, cb_ref,
                           w_ref, xo_ref, o_ref, buf_ref, halo_ref, sem):
    i = pl.program_id(0)
    slot = i % 2
    nxt = 1 - slot

    def gather(d_ref, s, r):
        return _row_copy(ys_ref, d_ref[0, 0, r], buf_ref.at[s], r, sem.at[s])

    def wait_rows(s):
        pltpu.make_async_copy(ys_ref.at[pl.ds(0, TM_IN)], buf_ref.at[s], sem.at[s]).wait()

    @pl.when(i == 0)
    def _():
        def body(r, c):
            gather(dcur_ref, 0, r).start()
            return c
        lax.fori_loop(0, TM_IN, body, 0, unroll=8)

    wait_rows(slot)
    x_new = x_ref[...] + buf_ref[slot]
    xo_ref[...] = x_new

    n_chunks = IN_COLS // COL_BLK
    per = -(-TM_IN // n_chunks)

    def issue(j):
        for r in range(j * per, min((j + 1) * per, TM_IN)):
            gather(dnext_ref, nxt, r).start()

    _reset_halo(halo_ref, tiles_per_seq)
    _project(x_new, g_ref, bg_ref, sgn_ref, cw_ref, cb_ref, w_ref, o_ref, halo_ref, between=issue)

    @pl.when(i == pl.num_programs(0) - 1)
    def _():
        wait_rows(nxt)


def _in_proj(x, p, seq, dest3=None, ys=None):
    n = x.shape[0]
    nt = n // TM_IN
    tiles_per_seq = seq // TM_IN

    def const(shape, **kw):
        return pl.BlockSpec(shape, lambda i: (0,) * len(shape), **kw)

    x_spec = pl.BlockSpec((TM_IN, D_MODEL), lambda i: (i, 0))
    params = [p["norm1_g"], p["b_gate_half"], p["sgu_norm_g"], p["conv_w"], p["conv_b"], p["w_in"]]
    param_specs = [const((1, D_MODEL)), const((1, GATE_COLS)), const((1, B_WIDTH)), const((CONV_WIDTH, C_WIDTH)),
                   const((1, C_WIDTH)), const((D_MODEL, IN_COLS), pipeline_mode=pl.Buffered(1))]
    o_spec = pl.BlockSpec((TM_IN, IN_COLS), lambda i: (i, 0))
    o_shape = jax.ShapeDtypeStruct((n, IN_COLS), BF16)
    halo = pltpu.VMEM((8, C_WIDTH), F32)
    if ys is None:
        return x, pl.pallas_call(
            functools.partial(_inproj_kernel, tiles_per_seq), grid=(nt,), in_specs=[x_spec] + param_specs,
            out_specs=o_spec, out_shape=o_shape, scratch_shapes=[halo],
            compiler_params=_cparams(("arbitrary",)), name="in_proj",
        )(x, *params)
    assert TM_IN == TT
    return pl.pallas_call(
        functools.partial(_inproj_combine_kernel, tiles_per_seq),
        grid=(nt,),
        in_specs=[pl.BlockSpec((1, 1, TT), lambda i: (i, 0, 0), memory_space=pltpu.SMEM),
                  pl.BlockSpec((1, 1, TT), lambda i: (jnp.minimum(i + 1, nt - 1), 0, 0), memory_space=pltpu.SMEM),
                  x_spec, pl.BlockSpec(memory_space=pl.ANY)] + param_specs,
        out_specs=[x_spec, o_spec],
        out_shape=[jax.ShapeDtypeStruct((n, D_MODEL), F32), o_shape],
        scratch_shapes=[pltpu.VMEM((2, TM_IN, D_MODEL), F32), halo, pltpu.SemaphoreType.DMA((2,))],
        compiler_params=_cparams(("arbitrary",)),
        name="in_proj_combine",
    )(dest3, dest3, x, ys, *params)


def _attn_kernel(q_ref, k0_ref, k1_ref, k2_ref, k3_ref, v0_ref, v1_ref, v2_ref, v3_ref, bias0_ref, bias1_ref, o_ref):
    lane = lax.broadcasted_iota(jnp.int32, (TQ, LANES), 1)
    first = lane < A_HEAD_DIM
    k_refs = (k0_ref, k1_ref, k2_ref, k3_ref)
    v_refs = (v0_ref, v1_ref, v2_ref, v3_ref)
    for j, bias_ref in enumerate((bias0_ref, bias1_ref)):
        rows = slice(j * TQ, (j + 1) * TQ)
        for p in range(A_HEADS // 2):
            sl = slice(p * LANES, (p + 1) * LANES)
            qp = q_ref[rows, sl]
            kp = jnp.concatenate([r[:, sl] for r in k_refs[j:j + 3]], axis=0)
            vp = jnp.concatenate([r[:, sl] for r in v_refs[j:j + 3]], axis=0)
            zero = jnp.zeros_like(qp)
            q2 = jnp.concatenate([jnp.where(first, qp, zero), jnp.where(first, zero, qp)], axis=0)
            s = lax.dot_general(q2, kp, (((1,), (1,)), ((), ())), preferred_element_type=F32)
            s = s + bias_ref[p]
            e = jnp.exp2(s - jnp.max(s, axis=-1, keepdims=True))
            l = jnp.sum(e, axis=-1, keepdims=True)
            o2 = jnp.dot(e.astype(BF16), vp, preferred_element_type=F32) / l
            o_ref[rows, sl] = jnp.where(first, o2[:TQ], o2[TQ:]).astype(BF16)


def _attention(proj, bias, batch, seq):
    nq = seq // TQ
    nstep = nq // 2

    def q_blk(col):
        return pl.BlockSpec((2 * TQ, COL_BLK), lambda b, i: (b * nstep + i, col))

    def kv_blk(col, off):
        return pl.BlockSpec((TQ, COL_BLK), lambda b, i: (b * nq + jnp.maximum(2 * i + off, 0), col))

    def bias_blk(j):
        return pl.BlockSpec((None, A_HEADS // 2, 2 * TQ, 3 * TQ), lambda b, i: (jnp.minimum(2 * i + j, 2), 0, 0, 0))

    return pl.pallas_call(
        _attn_kernel,
        grid=(batch, nstep),
        in_specs=[q_blk(Q_BLK)] + [kv_blk(K_BLK, off) for off in (-2, -1, 0, 1)]
        + [kv_blk(V_BLK, off) for off in (-2, -1, 0, 1)] + [bias_blk(0), bias_blk(1)],
        out_specs=pl.BlockSpec((2 * TQ, A_WIDTH), lambda b, i: (b * nstep + i, 0)),
        out_shape=jax.ShapeDtypeStruct((batch * seq, A_WIDTH), BF16),
        compiler_params=_cparams(("parallel", "parallel")),
        name="attention",
    )(*([proj] * 9), bias, bias)


def _attn_bias_table(rel_bias):
    heads = rel_bias.shape[0]
    rb = rel_bias.astype(F32) * LOG2E
    width = 4 * TQ
    n_head = 3 * TQ - 1 - REL_CLIP
    n_tail = width - n_head - (2 * REL_CLIP + 1)
    diag = jnp.concatenate([jnp.broadcast_to(rb[:, -1:], (heads, n_head)), rb[:, ::-1],
                            jnp.broadcast_to(rb[:, :1], (heads, n_tail))], axis=1)
    rolled = jnp.roll(diag, -(TQ - 1), axis=1)
    flat = jnp.broadcast_to(rolled[:, None, :], (heads, TQ, width)).reshape(heads, TQ * width)
    toe = flat[:, :TQ * (width - 1)].reshape(heads, TQ, width - 1)[:, :, :3 * TQ]
    q = jnp.arange(TQ)
    k = jnp.arange(3 * TQ)
    qc = (q // CHUNK)[:, None]
    kc = (k // CHUNK)[None, :]
    band = (kc >= qc) & (kc <= qc + A_LEFT_CHUNKS)
    variants = []
    for v in range(3):
        live = band & (k[None, :] >= (2 - v) * TQ)
        variants.append(jnp.where(live[None], toe, NEG))
    return jnp.stack(variants).reshape(3, heads // 2, 2 * TQ, 3 * TQ)


def _route(x_new, g2, wr_ref, br_ref, carry_ref):
    rows = x_new.shape[0]
    xn = _rms(x_new, g2)
    x_hi = xn.astype(BF16)
    x_lo = (xn - x_hi.astype(F32)).astype(BF16)
    p_hi = jnp.dot(x_hi, wr_ref[...], preferred_element_type=F32)
    p_lo = jnp.dot(x_lo, wr_ref[...], preferred_element_type=F32)
    lg = p_hi + pltpu.roll(p_hi, LANES - 32, 1) + p_lo + br_ref[...]
    lane = lax.broadcasted_iota(jnp.int32, (rows, LANES), 1).astype(F32)

    def top(mask):
        best = jnp.max(jnp.where(mask, lg, -jnp.inf), axis=-1, keepdims=True)
        idx = jnp.min(jnp.where(mask & (lg == best), lane, float(LANES)), axis=-1, keepdims=True)
        return best, idx

    gmask = lane < N_GROUPS
    gmax, grp = top(gmask)
    g_w = 1.0 / jnp.sum(jnp.where(gmask, jnp.exp(lg - gmax), 0.0), axis=-1, keepdims=True)
    lo = N_GROUPS + EXPERTS_PER_GROUP * grp
    in_grp = (lane >= lo) & (lane < lo + EXPERTS_PER_GROUP)
    l1, i1 = top(in_grp)
    l2, i2 = top(in_grp & (lane != i1))
    e2 = jnp.exp(l2 - l1)
    w1 = g_w / (1.0 + e2)
    w2 = g_w * e2 / (1.0 + e2)
    first_is_low = i1 < i2
    cw_a = jnp.where(first_is_low, w1, w2)
    cw_b = jnp.where(first_is_low, w2, w1)
    la = jnp.minimum(i1, i2) - lo
    lb = jnp.maximum(i1, i2) - lo
    bucket = grp * 6.0 + la * (7.0 - la) * 0.5 + (lb - la - 1.0)

    onehot = lane == bucket
    strict = (lax.broadcasted_iota(jnp.int32, (rows, rows), 0) > lax.broadcasted_iota(jnp.int32, (rows, rows), 1))
    prefix = jnp.dot(jnp.where(strict, 1.0, 0.0).astype(BF16), jnp.where(onehot, 1.0, 0.0).astype(BF16),
                     preferred_element_type=F32)
    carry = carry_ref[...]
    rank = jnp.sum(jnp.where(onehot, prefix + carry, 0.0), axis=-1, keepdims=True)
    carry = carry + jnp.sum(jnp.where(onehot, 1.0, 0.0), axis=0, keepdims=True)
    meta = jnp.where(lane == 0.0, bucket, jnp.where(lane == 1.0, rank, jnp.where(lane == 2.0, cw_a,
                     jnp.where(lane == 3.0, cw_b, 0.0))))
    return meta, carry


def _scan_linear(a, b, h_prev):
    n, c = a.shape
    groups = n // 8
    a = a.reshape(groups, 8, c)
    b = b.reshape(groups, 8, c)
    sub = lax.broadcasted_iota(jnp.int32, a.shape, 1)
    for s in (1, 2, 4):
        keep = sub >= s
        b = a * jnp.where(keep, pltpu.roll(b, s, 1), 0.0) + b
        a = a * jnp.where(keep, pltpu.roll(a, s, 1), 1.0)
    outs = []
    for g in range(groups):
        hg = a[g] * h_prev + b[g]
        outs.append(hg)
        h_prev = hg[7:8, :]
    return jnp.concatenate(outs, axis=0), h_prev


def _mix_kernel(gl_ref, u_ref, sv_ref, xg_ref, xc_ref, att_ref, x_ref,
                sgw_ref, sgb_ref, wax_ref, bax_ref, lam_ref,
                wbr_ref, wout_ref, g2_ref, wr_ref, br_ref,
                o_ref, meta_ref, cnt_ref, h_ref, carry_ref):
    t = pl.program_id(1)

    @pl.when((pl.program_id(0) == 0) & (t == 0))
    def _():
        carry_ref[...] = jnp.zeros_like(carry_ref)

    @pl.when(t == 0)
    def _():
        h_ref[...] = jnp.zeros_like(h_ref)

    u = u_ref[...].astype(F32)
    v = sv_ref[...]
    tri = (lax.broadcasted_iota(jnp.int32, (B_BLOCK, B_BLOCK), 0)
           >= lax.broadcasted_iota(jnp.int32, (B_BLOCK, B_BLOCK), 1))
    blocks = []
    for blk in range(TM_MIX // B_BLOCK):
        rs = slice(blk * B_BLOCK, (blk + 1) * B_BLOCK)
        groups = []
        for g in range(B_GROUPS):
            cs = slice(g * LANES, (g + 1) * LANES)
            w = jnp.where(tri, sgw_ref[g], 0.0).astype(BF16)
            groups.append(jnp.dot(w, v[rs, cs], preferred_element_type=F32))
        blocks.append(jnp.concatenate(groups, axis=1) + sgb_ref[...])
    b_out = (u * jnp.concatenate(blocks, axis=0)).astype(BF16)

    xc_bf = xc_ref[...]
    xc = xc_bf.astype(F32)
    ri = jnp.dot(xc_bf, wax_ref[...], preferred_element_type=F32) + bax_ref[...]
    r = _sigmoid(ri[:, :C_WIDTH])
    ig = _sigmoid(ri[:, C_WIDTH:])
    z = -lam_ref[...]
    softplus = jnp.maximum(z, 0.0) + jnp.log1p(jnp.exp(-jnp.abs(z)))
    log_a = -LRU_C * r * softplus
    a = jnp.exp(log_a)
    th = jnp.tanh(log_a)
    mult = jnp.sqrt(jnp.maximum(-2.0 * th / (1.0 - th), 0.0))
    h, h_ref[...] = _scan_linear(a, mult * (ig * xc), h_ref[...])
    c_out = (xg_ref[...].astype(F32) * h).astype(BF16)

    merged = None
    for k, br in enumerate((att_ref[...], b_out, c_out)):
        cs = slice(k * D_MODEL, (k + 1) * D_MODEL)
        term = gl_ref[:, cs] * jnp.dot(br, wbr_ref[k], preferred_element_type=F32).astype(BF16)
        merged = term if merged is None else merged + term
    x_new = x_ref[...] + jnp.dot(merged, wout_ref[...], preferred_element_type=F32)
    o_ref[...] = x_new

    meta, carry = _route(x_new, g2_ref[...], wr_ref, br_ref, carry_ref)
    carry_ref[...] = carry
    meta_ref[...] = meta
    cnt_ref[...] = jnp.broadcast_to(carry, cnt_ref.shape)


def _mix(proj, att, x, p, batch, seq):
    nt = seq // TM_MIX
    n = batch * seq

    def row(width, col):
        return pl.BlockSpec((TM_MIX, width), lambda b, t: (b * nt + t, col))

    def const(shape):
        return pl.BlockSpec(shape, lambda b, t: (0,) * len(shape))

    return pl.pallas_call(
        _mix_kernel,
        grid=(batch, nt),
        in_specs=[row(GATE_COLS, 0), row(COL_BLK, U_BLK), row(COL_BLK, SV_BLK), row(COL_BLK, XG_BLK),
                  row(COL_BLK, XR_BLK), row(A_WIDTH, 0), row(D_MODEL, 0),
                  const((B_GROUPS, B_BLOCK, B_BLOCK)), const((B_BLOCK, B_WIDTH)),
                  const((C_WIDTH, 2 * C_WIDTH)), const((1, 2 * C_WIDTH)), const((1, C_WIDTH)),
                  const((N_BRANCH, A_WIDTH, D_MODEL)), const((D_MODEL, D_MODEL)),
                  const((1, D_MODEL)), const((D_MODEL, LANES)), const((1, LANES))],
        out_specs=[row(D_MODEL, 0), row(LANES, 0), const((8, LANES))],
        out_shape=[jax.ShapeDtypeStruct((n, D_MODEL), F32),
                   jax.ShapeDtypeStruct((n, LANES), F32),
                   jax.ShapeDtypeStruct((8, LANES), F32)],
        scratch_shapes=[pltpu.VMEM((1, C_WIDTH), F32), pltpu.VMEM((1, LANES), F32)],
        compiler_params=_cparams(("arbitrary", "arbitrary")),
        name="mix",
    )(proj, proj, proj, proj, proj, att, x,
      p["sgu_w"], p["sgu_b"], p["w_ax"], p["b_ax"], p["lam"], p["w_branch"], p["w_out"], p["norm2_g"], p["w_r"], p["b_r"])


def _dispatch_kernel(d_ref, x_ref, g_ref, meta_ref, xs_in_ref, xs_ref, rows_ref, sem):
    del xs_in_ref
    rows_ref[:, :D_MODEL] = _rms(x_ref[...], g_ref[...])
    rows_ref[:, D_MODEL:] = meta_ref[...]

    for r in range(TT):
        _row_copy(rows_ref, r, xs_ref, d_ref[0, 0, r], sem).start()
    pltpu.make_async_copy(rows_ref, xs_ref.at[pl.ds(0, TT)], sem).wait()


def _dispatch(dest, x, g, meta, xs_init):
    n = x.shape[0]
    return pl.pallas_call(
        _dispatch_kernel,
        grid=(n // TT,),
        in_specs=[pl.BlockSpec((1, 1, TT), lambda i: (i, 0, 0), memory_space=pltpu.SMEM),
                  pl.BlockSpec((TT, D_MODEL), lambda i: (i, 0)),
                  pl.BlockSpec((1, D_MODEL), lambda i: (0, 0)),
                  pl.BlockSpec((TT, LANES), lambda i: (i, 0)),
                  pl.BlockSpec(memory_space=pl.ANY)],
        out_specs=pl.BlockSpec(memory_space=pl.ANY),
        out_shape=jax.ShapeDtypeStruct(xs_init.shape, F32),
        scratch_shapes=[pltpu.VMEM((TT, ROW_W), F32), pltpu.SemaphoreType.DMA(())],
        input_output_aliases={4: 0},
        compiler_params=_cparams(("arbitrary",)),
        name="dispatch",
    )(dest, x, g, meta, xs_init)


def _combine_kernel(d_ref, x_ref, ys_ref, g_ref, o_ref, buf_ref, sem):
    for r in range(TT):
        _row_copy(ys_ref, d_ref[0, 0, r], buf_ref, r, sem).start()
    pltpu.make_async_copy(ys_ref.at[pl.ds(0, TT)], buf_ref, sem).wait()
    o_ref[...] = _rms(x_ref[...] + buf_ref[...], g_ref[...])


def _combine(dest, x, ys, g):
    n = x.shape[0]
    return pl.pallas_call(
        _combine_kernel,
        grid=(n // TT,),
        in_specs=[pl.BlockSpec((1, 1, TT), lambda i: (i, 0, 0), memory_space=pltpu.SMEM),
                  pl.BlockSpec((TT, D_MODEL), lambda i: (i, 0)),
                  pl.BlockSpec(memory_space=pl.ANY),
                  pl.BlockSpec((1, D_MODEL), lambda i: (0, 0))],
        out_specs=pl.BlockSpec((TT, D_MODEL), lambda i: (i, 0)),
        out_shape=jax.ShapeDtypeStruct((n, D_MODEL), F32),
        scratch_shapes=[pltpu.VMEM((TT, D_MODEL), F32), pltpu.SemaphoreType.DMA(())],
        compiler_params=_cparams(("arbitrary",)),
        name="combine",
    )(dest, x, ys, g)


def _expert_kernel(ta_ref, tb_ref, tv_ref, tn_ref, xs_ref, ga_ref, ua_ref, da_ref, gb_ref, ub_ref, db_ref, o_ref,
                   wgu_ref, wd_ref):
    del ta_ref, tb_ref
    t = pl.program_id(0)

    @pl.when(tn_ref[t] != 0)
    def _():
        for e, (g_ref, u_ref, d_ref) in enumerate(((ga_ref, ua_ref, da_ref), (gb_ref, ub_ref, db_ref))):
            wgu_ref[e, :, :D_EXPERT] = g_ref[...].astype(BF16)
            wgu_ref[e, :, D_EXPERT:] = u_ref[...].astype(BF16)
            wd_ref[e] = d_ref[...].astype(BF16)

    @pl.when(tv_ref[t] == 0)
    def _():
        o_ref[...] = jnp.zeros_like(o_ref)

    @pl.when(tv_ref[t] != 0)
    def _():
        xb = xs_ref[:, :D_MODEL].astype(BF16)
        cw = xs_ref[:, D_MODEL:]

        def expert(e):
            gu = jnp.dot(xb, wgu_ref[e], preferred_element_type=F32)
            hid = gu[:, :D_EXPERT] * _sigmoid(gu[:, :D_EXPERT]) * gu[:, D_EXPERT:]
            return jnp.dot(hid.astype(BF16), wd_ref[e], preferred_element_type=F32)

        o_ref[...] = cw[:, 2:3] * expert(0) + cw[:, 3:4] * expert(1)


def _experts(tile_a, tile_b, tile_v, tile_new, xs, w_gate, w_up, w_down, layer):
    rows = xs.shape[0]

    def w_spec(rows_, cols_, which):
        return pl.BlockSpec((None, None, rows_, cols_),
                            lambda t, a, b, v, n: (layer, (a if which == 0 else b)[t], 0, 0))

    grid_spec = pltpu.PrefetchScalarGridSpec(
        num_scalar_prefetch=4,
        grid=(rows // TM_E,),
        in_specs=[pl.BlockSpec((TM_E, ROW_W), lambda t, a, b, v, n: (t, 0)),
                  w_spec(D_MODEL, D_EXPERT, 0), w_spec(D_MODEL, D_EXPERT, 0), w_spec(D_EXPERT, D_MODEL, 0),
                  w_spec(D_MODEL, D_EXPERT, 1), w_spec(D_MODEL, D_EXPERT, 1), w_spec(D_EXPERT, D_MODEL, 1)],
        out_specs=pl.BlockSpec((TM_E, D_MODEL), lambda t, a, b, v, n: (t, 0)),
        scratch_shapes=[pltpu.VMEM((2, D_MODEL, 2 * D_EXPERT), BF16), pltpu.VMEM((2, D_EXPERT, D_MODEL), BF16)],
    )
    return pl.pallas_call(
        _expert_kernel,
        grid_spec=grid_spec,
        out_shape=jax.ShapeDtypeStruct((rows, D_MODEL), F32),
        compiler_params=_cparams(("arbitrary",)),
        name="experts",
    )(tile_a, tile_b, tile_v, tile_new, xs, w_gate, w_up, w_down, w_gate, w_up, w_down)


def _plan(meta, counts, n_rows):
    bucket = meta[:, 0].astype(jnp.int32)
    rank = meta[:, 1].astype(jnp.int32)
    cnt = counts[0, :N_BUCKETS].astype(jnp.int32)
    padded = ((cnt + TM_E - 1) // TM_E) * TM_E
    ends = jnp.cumsum(padded)
    base = ends - padded
    ids = jnp.arange(N_BUCKETS, dtype=jnp.int32)
    dest = jnp.sum(jnp.where(bucket[:, None] == ids[None, :], base[None, :], 0), axis=1) + rank
    starts = jnp.arange(n_rows // TM_E, dtype=jnp.int32) * TM_E
    valid = starts < ends[-1]
    probe = jnp.minimum(starts, ends[-1] - 1)
    tile_bucket = jnp.minimum(jnp.sum((probe[:, None] >= ends[None, :]).astype(jnp.int32), axis=1), N_BUCKETS - 1)
    grp = tile_bucket // 6
    pair = tile_bucket % 6
    ge3 = (pair >= 3).astype(jnp.int32)
    ge5 = (pair >= 5).astype(jnp.int32)
    tile_a = grp * EXPERTS_PER_GROUP + ge3 + ge5
    tile_b = grp * EXPERTS_PER_GROUP + pair + 1 - 2 * ge3 - ge5
    tile_new = jnp.concatenate([jnp.ones((1,), jnp.int32), (tile_bucket[1:] != tile_bucket[:-1]).astype(jnp.int32)])
    return dest, tile_a, tile_b, valid.astype(jnp.int32), tile_new


def _block_diag(w):
    nb, d, _ = w.shape
    return jnp.einsum("hij,hg->higj", w, jnp.eye(nb, dtype=w.dtype)).reshape(nb * d, nb * d)


def kernel(x, norm1_g, w_in, b_gate, rel_bias, sgu_norm_g, sgu_w, sgu_b, conv_w, conv_b, rg_wa, rg_ba, rg_wx,
           rg_bx, rg_lambda, w_branch, w_out, norm2_g, router_grp_w, router_grp_b, router_exp_w, router_exp_b,
           exp_w_gate, exp_w_up, exp_w_down, final_norm_g):
    batch, seq, _ = x.shape
    depth = w_in.shape[0]
    n = batch * seq
    assert seq % (2 * TQ) == 0 and seq % TM_MIX == 0 and seq % TM_IN == 0 and n % TT == 0
    n_rows = n + N_BUCKETS * TM_E
    qkv_etc = 3 * A_WIDTH + 2 * B_WIDTH + 2 * C_WIDTH

    q_scale = (A_HEAD_DIM ** -0.5) * LOG2E
    w_in_r = jnp.concatenate([0.5 * w_in[:, :, qkv_etc:], q_scale * w_in[:, :, :A_WIDTH],
                              w_in[:, :, A_WIDTH:qkv_etc]], axis=2).astype(BF16)
    w_br = w_branch.astype(BF16)
    w_o = (0.5 * w_out).astype(BF16)
    n_logits = N_GROUPS + N_EXPERTS
    w_r32 = jnp.concatenate([router_grp_w, router_exp_w], axis=-1)
    w_r_hi = w_r32.astype(BF16)
    w_r_lo = (w_r32 - w_r_hi.astype(F32)).astype(BF16)
    zpad = lambda w, k: jnp.pad(w, ((0, 0), (0, 0), (0, k)))
    w_r = jnp.concatenate([zpad(w_r_hi, 32 - n_logits), zpad(w_r_lo, LANES - 32 - n_logits)], axis=-1)
    b_r = jnp.pad(jnp.concatenate([router_grp_b, router_exp_b], axis=-1), ((0, 0), (0, LANES - n_logits)))

    x2 = x.reshape(n, D_MODEL)
    xs = jnp.zeros((n_rows, ROW_W), F32)
    dest3 = ys = None
    for l in range(depth):
        p = {
            "norm1_g": norm1_g[l].reshape(1, D_MODEL),
            "b_gate_half": 0.5 * b_gate[l].reshape(1, GATE_COLS),
            "w_in": w_in_r[l],
            "sgu_norm_g": sgu_norm_g[l].reshape(1, B_WIDTH),
            "sgu_w": sgu_w[l],
            "sgu_b": jnp.repeat(sgu_b[l].T, LANES, axis=1),
            "conv_w": conv_w[l],
            "conv_b": conv_b[l].reshape(1, C_WIDTH),
            "w_ax": jnp.concatenate([_block_diag(rg_wa[l]), _block_diag(rg_wx[l])], axis=1).astype(BF16),
            "b_ax": jnp.concatenate([rg_ba[l], rg_bx[l]]).reshape(1, 2 * C_WIDTH),
            "lam": rg_lambda[l].reshape(1, C_WIDTH),
            "w_branch": w_br[l],
            "w_out": w_o[l],
            "norm2_g": norm2_g[l].reshape(1, D_MODEL),
            "w_r": w_r[l],
            "b_r": b_r[l].reshape(1, LANES),
        }
        x2, proj = _in_proj(x2, p, seq, dest3, ys)
        att = _attention(proj, _attn_bias_table(rel_bias[l]), batch, seq)
        x2, meta, counts = _mix(proj, att, x2, p, batch, seq)

        dest, tile_a, tile_b, tile_v, tile_new = _plan(meta, counts, n_rows)
        dest3 = dest.reshape(n // TT, 1, TT)
        xs = _dispatch(dest3, x2, p["norm2_g"], meta, xs)
        ys = _experts(tile_a, tile_b, tile_v, tile_new, xs, exp_w_gate, exp_w_up, exp_w_down, l)
    x2 = _combine(dest3, x2, ys, final_norm_g.reshape(1, D_MODEL))
    return x2.reshape(batch, seq, D_MODEL)
```

```python
import functools

import jax
import jax.numpy as jnp
from jax import lax
from jax.experimental import pallas as pl
from jax.experimental.pallas import tpu as pltpu

F32 = jnp.float32
BF16 = jnp.bfloat16

D_MODEL = 1024
CHUNK = 64
A_HEADS = 8
A_HEAD_DIM = 64
A_WIDTH = 512
A_LEFT_CHUNKS = 8
REL_CLIP = 128
B_BLOCK = 128
B_GROUPS = 4
B_WIDTH = 512
C_WIDTH = 512
C_BLOCKS = 8
CONV_WIDTH = 4
LRU_C = 8.0
N_BRANCH = 3
GATE_COLS = N_BRANCH * D_MODEL
IN_COLS = 3 * A_WIDTH + 2 * B_WIDTH + 2 * C_WIDTH + GATE_COLS
N_GROUPS = 4
EXPERTS_PER_GROUP = 4
N_EXPERTS = 16
D_EXPERT = 512
EPS = 1e-6
NEG = -1e30
LOG2E = 1.4426950408889634

COL_BLK = 512
Q_BLK, K_BLK, V_BLK, U_BLK, SV_BLK, XG_BLK, XR_BLK = 6, 7, 8, 9, 10, 11, 12

TM_IN = 512
TQ = 256
TM_MIX = 512
TT = 512
TM_E = 512
N_BUCKETS = N_GROUPS * 6
LANES = 128
ROW_W = D_MODEL + LANES
VMEM_LIMIT = 56 * 1024 * 1024


def _cparams(sem):
    return pltpu.CompilerParams(dimension_semantics=sem, vmem_limit_bytes=VMEM_LIMIT)


def _rms(x, g):
    return x * lax.rsqrt(jnp.mean(x * x, axis=-1, keepdims=True) + EPS) * g


def _sigmoid(x):
    return 0.5 * jnp.tanh(0.5 * x) + 0.5


def _row_copy(src_ref, src_row, dst_ref, dst_row, sem):
    return pltpu.make_async_copy(src_ref.at[pl.ds(src_row, 1)], dst_ref.at[pl.ds(dst_row, 1)], sem)


def _gelu(x):
    c = 0.7978845608028654
    h = 0.5 * x
    return h + h * jnp.tanh(x * (c + (c * 0.044715) * (x * x)))


def _project(x, g_ref, bg_ref, sgn_ref, cw_ref, cb_ref, w_ref, o_ref, halo_ref, between=None):
    xn = _rms(x, g_ref[...]).astype(BF16)
    heavy = [XR_BLK, SV_BLK, U_BLK, XG_BLK]
    order = heavy + [j for j in range(IN_COLS // COL_BLK) if j not in heavy]
    for step, j in enumerate(order):
        sl = slice(j * COL_BLK, (j + 1) * COL_BLK)
        acc = jnp.dot(xn, w_ref[:, sl], preferred_element_type=F32)
        if j < GATE_COLS // COL_BLK:
            acc = 1.0 + jnp.tanh(acc + bg_ref[:, sl])
        elif j in (U_BLK, XG_BLK):
            acc = _gelu(acc)
        elif j == SV_BLK:
            acc = _rms(_gelu(acc), sgn_ref[...])
        elif j == XR_BLK:
            ext = jnp.concatenate([halo_ref[...], acc], axis=0)
            halo_ref[...] = acc[TM_IN - 8:, :]
            conv = cb_ref[...]
            for k in range(CONV_WIDTH):
                lo = 8 - (CONV_WIDTH - 1) + k
                conv = conv + cw_ref[k:k + 1, :] * ext[lo:lo + TM_IN, :]
            acc = conv
        o_ref[:, sl] = acc.astype(BF16)
        if between is not None:
            between(step)


def _reset_halo(halo_ref, tiles_per_seq):
    @pl.when(pl.program_id(0) % tiles_per_seq == 0)
    def _():
        halo_ref[...] = jnp.zeros_like(halo_ref)


def _inproj_kernel(tiles_per_seq, x_ref, g_ref, bg_ref, sgn_ref, cw_ref, cb_ref, w_ref, o_ref, halo_ref):
    _reset_halo(halo_ref, tiles_per_seq)
    _project(x_ref[...], g_ref, bg_ref, sgn_ref, cw_ref, cb_ref, w_ref, o_ref, halo_ref)


def _inproj_combine_kernel(tiles_per_seq, dcur_ref, dnext_ref, x_ref, ys_ref, g_ref, bg_ref, sgn_ref, cw_ref, cb_ref,
                           w_ref, xo_ref, o_ref, buf_ref, halo_ref, sem):
    i = pl.program_id(0)
    slot = i % 2
    nxt = 1 - slot

    def gather(d_ref, s, r):
        return _row_copy(ys_ref, d_ref[0, 0, r], buf_ref.at[s], r, sem.at[s])

    def wait_rows(s):
        pltpu.make_async_copy(ys_ref.at[pl.ds(0, TM_IN)], buf_ref.at[s], sem.at[s]).wait()

    @pl.when(i == 0)
    def _():
        def body(r, c):
            gather(dcur_ref, 0, r).start()
            return c
        lax.fori_loop(0, TM_IN, body, 0, unroll=8)

    wait_rows(slot)
    x_new = x_ref[...] + buf_ref[slot]
    xo_ref[...] = x_new

    n_chunks = IN_COLS // COL_BLK
    per = -(-TM_IN // n_chunks)

    def issue(j):
        for r in range(j * per, min((j + 1) * per, TM_IN)):
            gather(dnext_ref, nxt, r).start()

    _reset_halo(halo_ref, tiles_per_seq)
    _project(x_new, g_ref, bg_ref, sgn_ref, cw_ref, cb_ref, w_ref, o_ref, halo_ref, between=issue)

    @pl.when(i == pl.num_programs(0) - 1)
    def _():
        wait_rows(nxt)


def _in_proj(x, p, seq, dest3=None, ys=None):
    n = x.shape[0]
    nt = n // TM_IN
    tiles_per_seq = seq // TM_IN

    def const(shape, **kw):
        return pl.BlockSpec(shape, lambda i: (0,) * len(shape), **kw)

    x_spec = pl.BlockSpec((TM_IN, D_MODEL), lambda i: (i, 0))
    params = [p["norm1_g"], p["b_gate_half"], p["sgu_norm_g"], p["conv_w"], p["conv_b"], p["w_in"]]
    param_specs = [const((1, D_MODEL)), const((1, GATE_COLS)), const((1, B_WIDTH)), const((CONV_WIDTH, C_WIDTH)),
                   const((1, C_WIDTH)), const((D_MODEL, IN_COLS), pipeline_mode=pl.Buffered(1))]
    o_spec = pl.BlockSpec((TM_IN, IN_COLS), lambda i: (i, 0))
    o_shape = jax.ShapeDtypeStruct((n, IN_COLS), BF16)
    halo = pltpu.VMEM((8, C_WIDTH), F32)
    if ys is None:
        return x, pl.pallas_call(
            functools.partial(_inproj_kernel, tiles_per_seq), grid=(nt,), in_specs=[x_spec] + param_specs,
            out_specs=o_spec, out_shape=o_shape, scratch_shapes=[halo],
            compiler_params=_cparams(("arbitrary",)), name="in_proj",
        )(x, *params)
    assert TM_IN == TT
    return pl.pallas_call(
        functools.partial(_inproj_combine_kernel, tiles_per_seq),
        grid=(nt,),
        in_specs=[pl.BlockSpec((1, 1, TT), lambda i: (i, 0, 0), memory_space=pltpu.SMEM),
                  pl.BlockSpec((1, 1, TT), lambda i: (jnp.minimum(i + 1, nt - 1), 0, 0), memory_space=pltpu.SMEM),
                  x_spec, pl.BlockSpec(memory_space=pl.ANY)] + param_specs,
        out_specs=[x_spec, o_spec],
        out_shape=[jax.ShapeDtypeStruct((n, D_MODEL), F32), o_shape],
        scratch_shapes=[pltpu.VMEM((2, TM_IN, D_MODEL), F32), halo, pltpu.SemaphoreType.DMA((2,))],
        compiler_params=_cparams(("arbitrary",)),
        name="in_proj_combine",
    )(dest3, dest3, x, ys, *params)


def _attn_kernel(q_ref, k0_ref, k1_ref, k2_ref, k3_ref, v0_ref, v1_ref, v2_ref, v3_ref, bias0_ref, bias1_ref, o_ref):
    lane = lax.broadcasted_iota(jnp.int32, (TQ, LANES), 1)
    first = lane < A_HEAD_DIM
    k_refs = (k0_ref, k1_ref, k2_ref, k3_ref)
    v_refs = (v0_ref, v1_ref, v2_ref, v3_ref)
    for j, bias_ref in enumerate((bias0_ref, bias1_ref)):
        rows = slice(j * TQ, (j + 1) * TQ)
        for p in range(A_HEADS // 2):
            sl = slice(p * LANES, (p + 1) * LANES)
            qp = q_ref[rows, sl]
            kp = jnp.concatenate([r[:, sl] for r in k_refs[j:j + 3]], axis=0)
            vp = jnp.concatenate([r[:, sl] for r in v_refs[j:j + 3]], axis=0)
            zero = jnp.zeros_like(qp)
            q2 = jnp.concatenate([jnp.where(first, qp, zero), jnp.where(first, zero, qp)], axis=0)
            s = lax.dot_general(q2, kp, (((1,), (1,)), ((), ())), preferred_element_type=F32)
            s = s + bias_ref[p]
            e = jnp.exp2(s - jnp.max(s, axis=-1, keepdims=True))
            l = jnp.sum(e, axis=-1, keepdims=True)
            o2 = jnp.dot(e.astype(BF16), vp, preferred_element_type=F32) / l
            o_ref[rows, sl] = jnp.where(first, o2[:TQ], o2[TQ:]).astype(BF16)


def _attention(proj, bias, batch, seq):
    nq = seq // TQ
    nstep = nq // 2

    def q_blk(col):
        return pl.BlockSpec((2 * TQ, COL_BLK), lambda b, i: (b * nstep + i, col))

    def kv_blk(col, off):
        return pl.BlockSpec((TQ, COL_BLK), lambda b, i: (b * nq + jnp.maximum(2 * i + off, 0), col))

    def bias_blk(j):
        return pl.BlockSpec((None, A_HEADS // 2, 2 * TQ, 3 * TQ), lambda b, i: (jnp.minimum(2 * i + j, 2), 0, 0, 0))

    return pl.pallas_call(
        _attn_kernel,
        grid=(batch, nstep),
        in_specs=[q_blk(Q_BLK)] + [kv_blk(K_BLK, off) for off in (-2, -1, 0, 1)]
        + [kv_blk(V_BLK, off) for off in (-2, -1, 0, 1)] + [bias_blk(0), bias_blk(1)],
        out_specs=pl.BlockSpec((2 * TQ, A_WIDTH), lambda b, i: (b * nstep + i, 0)),
        out_shape=jax.ShapeDtypeStruct((batch * seq, A_WIDTH), BF16),
        compiler_params=_cparams(("parallel", "parallel")),
        name="attention",
    )(*([proj] * 9), bias, bias)


def _attn_bias_table(rel_bias):
    heads = rel_bias.shape[0]
    rb = rel_bias.astype(F32) * LOG2E
    width = 4 * TQ
    n_head = 3 * TQ - 1 - REL_CLIP
    n_tail = width - n_head - (2 * REL_CLIP + 1)
    diag = jnp.concatenate([jnp.broadcast_to(rb[:, -1:], (heads, n_head)), rb[:, ::-1],
                            jnp.broadcast_to(rb[:, :1], (heads, n_tail))], axis=1)
    rolled = jnp.roll(diag, -(TQ - 1), axis=1)
    flat = jnp.broadcast_to(rolled[:, None, :], (heads, TQ, width)).reshape(heads, TQ * width)
    toe = flat[:, :TQ * (width - 1)].reshape(heads, TQ, width - 1)[:, :, :3 * TQ]
    q = jnp.arange(TQ)
    k = jnp.arange(3 * TQ)
    qc = (q // CHUNK)[:, None]
    kc = (k // CHUNK)[None, :]
    band = (kc >= qc) & (kc <= qc + A_LEFT_CHUNKS)
    variants = []
    for v in range(3):
        live = band & (k[None, :] >= (2 - v) * TQ)
        variants.append(jnp.where(live[None], toe, NEG))
    return jnp.stack(variants).reshape(3, heads // 2, 2 * TQ, 3 * TQ)


def _route(x_new, g2, wr_ref, br_ref, carry_ref):
    rows = x_new.shape[0]
    xn = _rms(x_new, g2)
    x_hi = xn.astype(BF16)
    x_lo = (xn - x_hi.astype(F32)).astype(BF16)
    p_hi = jnp.dot(x_hi, wr_ref[...], preferred_element_type=F32)
    p_lo = jnp.dot(x_lo, wr_ref[...], preferred_element_type=F32)
    lg = p_hi + pltpu.roll(p_hi, LANES - 32, 1) + p_lo + br_ref[...]
    lane = lax.broadcasted_iota(jnp.int32, (rows, LANES), 1).astype(F32)

    def top(mask):
        best = jnp.max(jnp.where(mask, lg, -jnp.inf), axis=-1, keepdims=True)
        idx = jnp.min(jnp.where(mask & (lg == best), lane, float(LANES)), axis=-1, keepdims=True)
        return best, idx

    gmask = lane < N_GROUPS
    gmax, grp = top(gmask)
    g_w = 1.0 / jnp.sum(jnp.where(gmask, jnp.exp(lg - gmax), 0.0), axis=-1, keepdims=True)
    lo = N_GROUPS + EXPERTS_PER_GROUP * grp
    in_grp = (lane >= lo) & (lane < lo + EXPERTS_PER_GROUP)
    l1, i1 = top(in_grp)
    l2, i2 = top(in_grp & (lane != i1))
    e2 = jnp.exp(l2 - l1)
    w1 = g_w / (1.0 + e2)
    w2 = g_w * e2 / (1.0 + e2)
    first_is_low = i1 < i2
    cw_a = jnp.where(first_is_low, w1, w2)
    cw_b = jnp.where(first_is_low, w2, w1)
    la = jnp.minimum(i1, i2) - lo
    lb = jnp.maximum(i1, i2) - lo
    bucket = grp * 6.0 + la * (7.0 - la) * 0.5 + (lb - la - 1.0)

    onehot = lane == bucket
    strict = (lax.broadcasted_iota(jnp.int32, (rows, rows), 0) > lax.broadcasted_iota(jnp.int32, (rows, rows), 1))
    prefix = jnp.dot(jnp.where(strict, 1.0, 0.0).astype(BF16), jnp.where(onehot, 1.0, 0.0).astype(BF16),
                     preferred_element_type=F32)
    carry = carry_ref[...]
    rank = jnp.sum(jnp.where(onehot, prefix + carry, 0.0), axis=-1, keepdims=True)
    carry = carry + jnp.sum(jnp.where(onehot, 1.0, 0.0), axis=0, keepdims=True)
    meta = jnp.where(lane == 0.0, bucket, jnp.where(lane == 1.0, rank, jnp.where(lane == 2.0, cw_a,
                     jnp.where(lane == 3.0, cw_b, 0.0))))
    return meta, carry


def _scan_linear(a, b, h_prev):
    n, c = a.shape
    groups = n // 8
    a = a.reshape(groups, 8, c)
    b = b.reshape(groups, 8, c)
    sub = lax.broadcasted_iota(jnp.int32, a.shape, 1)
    for s in (1, 2, 4):
        keep = sub >= s
        b = a * jnp.where(keep, pltpu.roll(b, s, 1), 0.0) + b
        a = a * jnp.where(keep, pltpu.roll(a, s, 1), 1.0)
    outs = []
    for g in range(groups):
        hg = a[g] * h_prev + b[g]
        outs.append(hg)
        h_prev = hg[7:8, :]
    return jnp.concatenate(outs, axis=0), h_prev


def _mix_kernel(gl_ref, u_ref, sv_ref, xg_ref, xc_ref, att_ref, x_ref,
                sgw_ref, sgb_ref, wax_ref, bax_ref, lam_ref,
                wbr_ref, wout_ref, g2_ref, wr_ref, br_ref,
                o_ref, meta_ref, cnt_ref, h_ref, carry_ref):
    t = pl.program_id(1)

    @pl.when((pl.program_id(0) == 0) & (t == 0))
    def _():
        carry_ref[...] = jnp.zeros_like(carry_ref)

    @pl.when(t == 0)
    def _():
        h_ref[...] = jnp.zeros_like(h_ref)

    u = u_ref[...].astype(F32)
    v = sv_ref[...]
    tri = (lax.broadcasted_iota(jnp.int32, (B_BLOCK, B_BLOCK), 0)
           >= lax.broadcasted_iota(jnp.int32, (B_BLOCK, B_BLOCK), 1))
    blocks = []
    for blk in range(TM_MIX // B_BLOCK):
        rs = slice(blk * B_BLOCK, (blk + 1) * B_BLOCK)
        groups = []
        for g in range(B_GROUPS):
            cs = slice(g * LANES, (g + 1) * LANES)
            w = jnp.where(tri, sgw_ref[g], 0.0).astype(BF16)
            groups.append(jnp.dot(w, v[rs, cs], preferred_element_type=F32))
        blocks.append(jnp.concatenate(groups, axis=1) + sgb_ref[...])
    b_out = (u * jnp.concatenate(blocks, axis=0)).astype(BF16)

    xc_bf = xc_ref[...]
    xc = xc_bf.astype(F32)
    ri = jnp.dot(xc_bf, wax_ref[...], preferred_element_type=F32) + bax_ref[...]
    r = _sigmoid(ri[:, :C_WIDTH])
    ig = _sigmoid(ri[:, C_WIDTH:])
    z = -lam_ref[...]
    softplus = jnp.maximum(z, 0.0) + jnp.log1p(jnp.exp(-jnp.abs(z)))
    log_a = -LRU_C * r * softplus
    a = jnp.exp(log_a)
    th = jnp.tanh(log_a)
    mult = jnp.sqrt(jnp.maximum(-2.0 * th / (1.0 - th), 0.0))
    h, h_ref[...] = _scan_linear(a, mult * (ig * xc), h_ref[...])
    c_out = (xg_ref[...].astype(F32) * h).astype(BF16)

    merged = None
    for k, br in enumerate((att_ref[...], b_out, c_out)):
        cs = slice(k * D_MODEL, (k + 1) * D_MODEL)
        term = gl_ref[:, cs] * jnp.dot(br, wbr_ref[k], preferred_element_type=F32).astype(BF16)
        merged = term if merged is None else merged + term
    x_new = x_ref[...] + jnp.dot(merged, wout_ref[...], preferred_element_type=F32)
    o_ref[...] = x_new

    meta, carry = _route(x_new, g2_ref[...], wr_ref, br_ref, carry_ref)
    carry_ref[...] = carry
    meta_ref[...] = meta
    cnt_ref[...] = jnp.broadcast_to(carry, cnt_ref.shape)


def _mix(proj, att, x, p, batch, seq):
    nt = seq // TM_MIX
    n = batch * seq

    def row(width, col):
        return pl.BlockSpec((TM_MIX, width), lambda b, t: (b * nt + t, col))

    def const(shape):
        return pl.BlockSpec(shape, lambda b, t: (0,) * len(shape))

    return pl.pallas_call(
        _mix_kernel,
        grid=(batch, nt),
        in_specs=[row(GATE_COLS, 0), row(COL_BLK, U_BLK), row(COL_BLK, SV_BLK), row(COL_BLK, XG_BLK),
                  row(COL_BLK, XR_BLK), row(A_WIDTH, 0), row(D_MODEL, 0),
                  const((B_GROUPS, B_BLOCK, B_BLOCK)), const((B_BLOCK, B_WIDTH)),
                  const((C_WIDTH, 2 * C_WIDTH)), const((1, 2 * C_WIDTH)), const((1, C_WIDTH)),
                  const((N_BRANCH, A_WIDTH, D_MODEL)), const((D_MODEL, D_MODEL)),
                  const((1, D_MODEL)), const((D_MODEL, LANES)), const((1, LANES))],
        out_specs=[row(D_MODEL, 0), row(LANES, 0), const((8, LANES))],
        out_shape=[jax.ShapeDtypeStruct((n, D_MODEL), F32),
                   jax.ShapeDtypeStruct((n, LANES), F32),
                   jax.ShapeDtypeStruct((8, LANES), F32)],
        scratch_shapes=[pltpu.VMEM((1, C_WIDTH), F32), pltpu.VMEM((1, LANES), F32)],
        compiler_params=_cparams(("arbitrary", "arbitrary")),
        name="mix",
    )(proj, proj, proj, proj, proj, att, x,
      p["sgu_w"], p["sgu_b"], p["w_ax"], p["b_ax"], p["lam"], p["w_branch"], p["w_out"], p["norm2_g"], p["w_r"], p["b_r"])


def _dispatch_kernel(d_ref, x_ref, g_ref, meta_ref, xs_in_ref, xs_ref, rows_ref, sem):
    del xs_in_ref
    rows_ref[:, :D_MODEL] = _rms(x_ref[...], g_ref[...])
    rows_ref[:, D_MODEL:] = meta_ref[...]

    for r in range(TT):
        _row_copy(rows_ref, r, xs_ref, d_ref[0, 0, r], sem).start()
    pltpu.make_async_copy(rows_ref, xs_ref.at[pl.ds(0, TT)], sem).wait()


def _dispatch(dest, x, g, meta, xs_init):
    n = x.shape[0]
    return pl.pallas_call(
        _dispatch_kernel,
        grid=(n // TT,),
        in_specs=[pl.BlockSpec((1, 1, TT), lambda i: (i, 0, 0), memory_space=pltpu.SMEM),
                  pl.BlockSpec((TT, D_MODEL), lambda i: (i, 0)),
                  pl.BlockSpec((1, D_MODEL), lambda i: (0, 0)),
                  pl.BlockSpec((TT, LANES), lambda i: (i, 0)),
                  pl.BlockSpec(memory_space=pl.ANY)],
        out_specs=pl.BlockSpec(memory_space=pl.ANY),
        out_shape=jax.ShapeDtypeStruct(xs_init.shape, F32),
        scratch_shapes=[pltpu.VMEM((TT, ROW_W), F32), pltpu.SemaphoreType.DMA(())],
        input_output_aliases={4: 0},
        compiler_params=_cparams(("arbitrary",)),
        name="dispatch",
    )(dest, x, g, meta, xs_init)


def _combine_kernel(d_ref, x_ref, ys_ref, g_ref, o_ref, buf_ref, sem):
    for r in range(TT):
        _row_copy(ys_ref, d_ref[0, 0, r], buf_ref, r, sem).start()
    pltpu.make_async_copy(ys_ref.at[pl.ds(0, TT)], buf_ref, sem).wait()
    o_ref[...] = _rms(x_ref[...] + buf_ref[...], g_ref[...])


def _combine(dest, x, ys, g):
    n = x.shape[0]
    return pl.pallas_call(
        _combine_kernel,
        grid=(n // TT,),
        in_specs=[pl.BlockSpec((1, 1, TT), lambda i: (i, 0, 0), memory_space=pltpu.SMEM),
                  pl.BlockSpec((TT, D_MODEL), lambda i: (i, 0)),
                  pl.BlockSpec(memory_space=pl.ANY),
                  pl.BlockSpec((1, D_MODEL), lambda i: (0, 0))],
        out_specs=pl.BlockSpec((TT, D_MODEL), lambda i: (i, 0)),
        out_shape=jax.ShapeDtypeStruct((n, D_MODEL), F32),
        scratch_shapes=[pltpu.VMEM((TT, D_MODEL), F32), pltpu.SemaphoreType.DMA(())],
        compiler_params=_cparams(("arbitrary",)),
        name="combine",
    )(dest, x, ys, g)


def _expert_kernel(ta_ref, tb_ref, tv_ref, tn_ref, xs_ref, ga_ref, ua_ref, da_ref, gb_ref, ub_ref, db_ref, o_ref,
                   wgu_ref, wd_ref):
    del ta_ref, tb_ref
    t = pl.program_id(0)

    @pl.when(tn_ref[t] != 0)
    def _():
        for e, (g_ref, u_ref, d_ref) in enumerate(((ga_ref, ua_ref, da_ref), (gb_ref, ub_ref, db_ref))):
            wgu_ref[e, :, :D_EXPERT] = g_ref[...].astype(BF16)
            wgu_ref[e, :, D_EXPERT:] = u_ref[...].astype(BF16)
            wd_ref[e] = d_ref[...].astype(BF16)

    @pl.when(tv_ref[t] == 0)
    def _():
        o_ref[...] = jnp.zeros_like(o_ref)

    @pl.when(tv_ref[t] != 0)
    def _():
        xb = xs_ref[:, :D_MODEL].astype(BF16)
        cw = xs_ref[:, D_MODEL:]

        def expert(e):
            gu = jnp.dot(xb, wgu_ref[e], preferred_element_type=F32)
            hid = gu[:, :D_EXPERT] * _sigmoid(gu[:, :D_EXPERT]) * gu[:, D_EXPERT:]
            return jnp.dot(hid.astype(BF16), wd_ref[e], preferred_element_type=F32)

        o_ref[...] = cw[:, 2:3] * expert(0) + cw[:, 3:4] * expert(1)


def _experts(tile_a, tile_b, tile_v, tile_new, xs, w_gate, w_up, w_down, layer):
    rows = xs.shape[0]

    def w_spec(rows_, cols_, which):
        return pl.BlockSpec((None, None, rows_, cols_),
                            lambda t, a, b, v, n: (layer, (a if which == 0 else b)[t], 0, 0))

    grid_spec = pltpu.PrefetchScalarGridSpec(
        num_scalar_prefetch=4,
        grid=(rows // TM_E,),
        in_specs=[pl.BlockSpec((TM_E, ROW_W), lambda t, a, b, v, n: (t, 0)),
                  w_spec(D_MODEL, D_EXPERT, 0), w_spec(D_MODEL, D_EXPERT, 0), w_spec(D_EXPERT, D_MODEL, 0),
                  w_spec(D_MODEL, D_EXPERT, 1), w_spec(D_MODEL, D_EXPERT, 1), w_spec(D_EXPERT, D_MODEL, 1)],
        out_specs=pl.BlockSpec((TM_E, D_MODEL), lambda t, a, b, v, n: (t, 0)),
        scratch_shapes=[pltpu.VMEM((2, D_MODEL, 2 * D_EXPERT), BF16), pltpu.VMEM((2, D_EXPERT, D_MODEL), BF16)],
    )
    return pl.pallas_call(
        _expert_kernel,
        grid_spec=grid_spec,
        out_shape=jax.ShapeDtypeStruct((rows, D_MODEL), F32),
        compiler_params=_cparams(("arbitrary",)),
        name="experts",
    )(tile_a, tile_b, tile_v, tile_new, xs, w_gate, w_up, w_down, w_gate, w_up, w_down)


def _plan(meta, counts, n_rows):
    bucket = meta[:, 0].astype(jnp.int32)
    rank = meta[:, 1].astype(jnp.int32)
    cnt = counts[0, :N_BUCKETS].astype(jnp.int32)
    padded = ((cnt + TM_E - 1) // TM_E) * TM_E
    ends = jnp.cumsum(padded)
    base = ends - padded
    ids = jnp.arange(N_BUCKETS, dtype=jnp.int32)
    dest = jnp.sum(jnp.where(bucket[:, None] == ids[None, :], base[None, :], 0), axis=1) + rank
    starts = jnp.arange(n_rows // TM_E, dtype=jnp.int32) * TM_E
    valid = starts < ends[-1]
    probe = jnp.minimum(starts, ends[-1] - 1)
    tile_bucket = jnp.minimum(jnp.sum((probe[:, None] >= ends[None, :]).astype(jnp.int32), axis=1), N_BUCKETS - 1)
    grp = tile_bucket // 6
    pair = tile_bucket % 6
    ge3 = (pair >= 3).astype(jnp.int32)
    ge5 = (pair >= 5).astype(jnp.int32)
    tile_a = grp * EXPERTS_PER_GROUP + ge3 + ge5
    tile_b = grp * EXPERTS_PER_GROUP + pair + 1 - 2 * ge3 - ge5
    tile_new = jnp.concatenate([jnp.ones((1,), jnp.int32), (tile_bucket[1:] != tile_bucket[:-1]).astype(jnp.int32)])
    return dest, tile_a, tile_b, valid.astype(jnp.int32), tile_new


def _block_diag(w):
    nb, d, _ = w.shape
    return jnp.einsum("hij,hg->higj", w, jnp.eye(nb, dtype=w.dtype)).reshape(nb * d, nb * d)


def kernel(x, norm1_g, w_in, b_gate, rel_bias, sgu_norm_g, sgu_w, sgu_b, conv_w, conv_b, rg_wa, rg_ba, rg_wx,
           rg_bx, rg_lambda, w_branch, w_out, norm2_g, router_grp_w, router_grp_b, router_exp_w, router_exp_b,
           exp_w_gate, exp_w_up, exp_w_down, final_norm_g):
    batch, seq, _ = x.shape
    depth = w_in.shape[0]
    n = batch * seq
    assert seq % (2 * TQ) == 0 and seq % TM_MIX == 0 and seq % TM_IN == 0 and n % TT == 0
    n_rows = n + N_BUCKETS * TM_E
    qkv_etc = 3 * A_WIDTH + 2 * B_WIDTH + 2 * C_WIDTH

    q_scale = (A_HEAD_DIM ** -0.5) * LOG2E
    w_in_r = jnp.concatenate([0.5 * w_in[:, :, qkv_etc:], q_scale * w_in[:, :, :A_WIDTH],
                              w_in[:, :, A_WIDTH:qkv_etc]], axis=2).astype(BF16)
    w_br = w_branch.astype(BF16)
    w_o = (0.5 * w_out).astype(BF16)
    n_logits = N_GROUPS + N_EXPERTS
    w_r32 = jnp.concatenate([router_grp_w, router_exp_w], axis=-1)
    w_r_hi = w_r32.astype(BF16)
    w_r_lo = (w_r32 - w_r_hi.astype(F32)).astype(BF16)
    zpad = lambda w, k: jnp.pad(w, ((0, 0), (0, 0), (0, k)))
    w_r = jnp.concatenate([zpad(w_r_hi, 32 - n_logits), zpad(w_r_lo, LANES - 32 - n_logits)], axis=-1)
    b_r = jnp.pad(jnp.concatenate([router_grp_b, router_exp_b], axis=-1), ((0, 0), (0, LANES - n_logits)))

    x2 = x.reshape(n, D_MODEL)
    xs = jnp.zeros((n_rows, ROW_W), F32)
    dest3 = ys = None
    for l in range(depth):
        p = {
            "norm1_g": norm1_g[l].reshape(1, D_MODEL),
            "b_gate_half": 0.5 * b_gate[l].reshape(1, GATE_COLS),
            "w_in": w_in_r[l],
            "sgu_norm_g": sgu_norm_g[l].reshape(1, B_WIDTH),
            "sgu_w": sgu_w[l],
            "sgu_b": jnp.repeat(sgu_b[l].T, LANES, axis=1),
            "conv_w": conv_w[l],
            "conv_b": conv_b[l].reshape(1, C_WIDTH),
            "w_ax": jnp.concatenate([_block_diag(rg_wa[l]), _block_diag(rg_wx[l])], axis=1).astype(BF16),
            "b_ax": jnp.concatenate([rg_ba[l], rg_bx[l]]).reshape(1, 2 * C_WIDTH),
            "lam": rg_lambda[l].reshape(1, C_WIDTH),
            "w_branch": w_br[l],
            "w_out": w_o[l],
            "norm2_g": norm2_g[l].reshape(1, D_MODEL),
            "w_r": w_r[l],
            "b_r": b_r[l].reshape(1, LANES),
        }
        x2, proj = _in_proj(x2, p, seq, dest3, ys)
        att = _attention(proj, _attn_bias_table(rel_bias[l]), batch, seq)
        x2, meta, counts = _mix(proj, att, x2, p, batch, seq)

        dest, tile_a, tile_b, tile_v, tile_new = _plan(meta, counts, n_rows)
        dest3 = dest.reshape(n // TT, 1, TT)
        xs = _dispatch(dest3, x2, p["norm2_g"], meta, xs)
        ys = _experts(tile_a, tile_b, tile_v, tile_new, xs, exp_w_gate, exp_w_up, exp_w_down, l)
    x2 = _combine(dest3, x2, ys, final_norm_g.reshape(1, D_MODEL))
    return x2.reshape(batch, seq, D_MODEL)
```
